```python
import math
import jax
import jax.numpy as jnp
from jax import lax
import numpy as np

D_MODEL = 1024
BATCH = 8
SEQ = 4096
DEPTH = 2

GRID_W = 64
CTX_LEN = 256
HEAD_DIM = 64
GROUP_WIDTH = D_MODEL // 4
D_MIX = 4 * GROUP_WIDTH
EPS = 1e-6
NEG_INF = -1e30
ROPE_BASE = 10000.0

A_HEADS = GROUP_WIDTH // HEAD_DIM
A_KV_HEADS = A_HEADS // 2
A_WINDOW = 128
A_BLOCK = 128

B_HEADS = GROUP_WIDTH // HEAD_DIM
B_NOPE = HEAD_DIM
B_ROPE = HEAD_DIM // 2
B_VDIM = HEAD_DIM
B_Q_LORA = 3 * GROUP_WIDTH // 4
B_KV_LORA = GROUP_WIDTH // 2
MLA_BLOCK = 128

C_CH = GROUP_WIDTH
C_GROUP = 16
C_NGROUPS = C_CH // C_GROUP
C_STATE = 64
STEP_MIN = 0.001
STEP_MAX = 0.1

D_HEADS = GROUP_WIDTH // HEAD_DIM
NA_KH = 8
NA_KW = 16

PEER_HEADS = 8
PEER_NKEYS = 128
PEER_NEXPERTS = PEER_NKEYS * PEER_NKEYS
PEER_DKEY = 256
PEER_TOPK = 16
PEER_CHUNK = 128

IN_SIZES = (A_HEADS * HEAD_DIM, A_KV_HEADS * HEAD_DIM, A_KV_HEADS * HEAD_DIM,
            B_Q_LORA, B_KV_LORA, B_ROPE,
            C_CH,
            D_HEADS * HEAD_DIM, D_HEADS * HEAD_DIM, D_HEADS * HEAD_DIM)
IN_WIDTH = sum(IN_SIZES)

kernel_name = 'hybrid_dit_block'


def rmsnorm(x, g):
    xf = x.astype(jnp.float32)
    y = xf * lax.rsqrt(jnp.mean(xf * xf, axis=-1, keepdims=True) + EPS)
    return (y * g.astype(jnp.float32)).astype(x.dtype)


def group_rmsnorm(y, g):
    shp = y.shape
    yf = y.astype(jnp.float32).reshape(shp[:-1] + (4, GROUP_WIDTH))
    yf = yf * lax.rsqrt(jnp.mean(yf * yf, axis=-1, keepdims=True) + EPS)
    return (yf.reshape(shp) * g.astype(jnp.float32)).astype(y.dtype)


def modulate(x, g, shift, scale):
    return rmsnorm(x, g) * (1 + scale) + shift


def split_cols(z):
    offs = np.cumsum(IN_SIZES)[:-1].tolist()
    return jnp.split(z, offs, axis=-1)


def joint_softmax(parts):
    sizes = [p.shape[-1] for p in parts]
    probs = jax.nn.softmax(jnp.concatenate(parts, axis=-1), axis=-1)
    return jnp.split(probs, np.cumsum(sizes)[:-1].tolist(), axis=-1)


def _rope_axis(x, pos):
    half = x.shape[-1] // 2
    inv_freq = ROPE_BASE ** (-jnp.arange(half, dtype=jnp.float32) / half)
    ang = pos.astype(jnp.float32)[:, None] * inv_freq[None, :]
    cos = jnp.cos(ang)[:, None, :]
    sin = jnp.sin(ang)[:, None, :]
    xf = x.astype(jnp.float32)
    x1, x2 = xf[..., :half], xf[..., half:]
    return jnp.concatenate([x1 * cos - x2 * sin, x1 * sin + x2 * cos], axis=-1)


def axial_rope(x, row_pos, col_pos):
    a = x.shape[-1] // 2
    return jnp.concatenate([_rope_axis(x[..., :a], row_pos), _rope_axis(x[..., a:], col_pos)], axis=-1).astype(x.dtype)


def dense_attention(q, k, v):
    s = jnp.einsum('bqhd,bkhd->bhqk', q, k).astype(jnp.float32) * (q.shape[-1] ** -0.5)
    p = jax.nn.softmax(s, axis=-1).astype(v.dtype)
    o = jnp.einsum('bhqk,bkhd->bqhd', p, v)
    return o.reshape(o.shape[0], o.shape[1], -1)


def swa_mixer(q, k, v, qc, kc, vc, sink):
    B_, L = q.shape[:2]
    nb = L // A_BLOCK
    rep = A_HEADS // A_KV_HEADS
    scale = HEAD_DIM ** -0.5
    sink_l = sink.astype(jnp.float32).reshape(A_KV_HEADS, rep, 1, 1)

    def band_blocks(t):
        tp = jnp.pad(t, ((0, 0), (A_BLOCK, A_BLOCK), (0, 0), (0, 0)))
        tp = tp.reshape(B_, nb + 2, A_BLOCK, A_KV_HEADS, HEAD_DIM)
        return jnp.concatenate([tp[:, :-2], tp[:, 1:-1], tp[:, 2:]], axis=2)

    kw, vw = band_blocks(k), band_blocks(v)
    qb = q.reshape(B_, nb, A_BLOCK, A_KV_HEADS, rep, HEAD_DIM)
    s_loc = jnp.einsum('bnqgrd,bnkgd->bngrqk', qb, kw).astype(jnp.float32) * scale
    s_ctx = jnp.einsum('bnqgrd,bkgd->bngrqk', qb, kc).astype(jnp.float32) * scale
    qi = jnp.arange(A_BLOCK)[:, None]
    kj = jnp.arange(3 * A_BLOCK)[None, :]
    key_pos = (jnp.arange(nb)[:, None, None] - 1) * A_BLOCK + kj[None]
    valid = (jnp.abs(qi + A_BLOCK - kj) <= A_WINDOW)[None] & (key_pos >= 0) & (key_pos < L)
    s_loc = jnp.where(valid[None, :, None, None], s_loc, NEG_INF)
    sink_b = jnp.broadcast_to(sink_l, s_loc.shape[:-1] + (1,))
    p_loc, p_ctx, _ = joint_softmax([s_loc, s_ctx, sink_b])
    o = (jnp.einsum('bngrqk,bnkgd->bnqgrd', p_loc.astype(v.dtype), vw)
         + jnp.einsum('bngrqk,bkgd->bnqgrd', p_ctx.astype(vc.dtype), vc))
    y = o.reshape(B_, L, A_HEADS * HEAD_DIM)
    yc = None
    if qc is not None:
        Lc = qc.shape[1]
        qcb = qc.reshape(B_, Lc, A_KV_HEADS, rep, HEAD_DIM)
        s = jnp.einsum('bqgrd,bkgd->bgrqk', qcb, kc).astype(jnp.float32) * scale
        p_c, _ = joint_softmax([s, jnp.broadcast_to(sink_l, s.shape[:-1] + (1,))])
        yc = jnp.einsum('bgrqk,bkgd->bqgrd', p_c.astype(vc.dtype), vc).reshape(B_, Lc, A_HEADS * HEAD_DIM)
    return y, yc


def mla_q(cq, qn_g, w_uq, row_pos, col_pos):
    B_, L = cq.shape[:2]
    q = (rmsnorm(cq, qn_g) @ w_uq).reshape(B_, L, B_HEADS, B_NOPE + B_ROPE)
    q_nope, q_rope = q[..., :B_NOPE], q[..., B_NOPE:]
    if row_pos is not None:
        q_rope = axial_rope(q_rope, row_pos, col_pos)
    return jnp.concatenate([q_nope, q_rope], axis=-1)


def mla_kv(ckv, kr, kvn_g, w_ukv, row_pos, col_pos):
    B_, L = ckv.shape[:2]
    kv = (rmsnorm(ckv, kvn_g) @ w_ukv).reshape(B_, L, B_HEADS, B_NOPE + B_VDIM)
    k_nope, v = kv[..., :B_NOPE], kv[..., B_NOPE:]
    k_rope = kr[:, :, None, :]
    if row_pos is not None:
        k_rope = axial_rope(k_rope, row_pos, col_pos)
    k = jnp.concatenate([k_nope, jnp.broadcast_to(k_rope, (B_, L, B_HEADS, B_ROPE))], axis=-1)
    return k, v


def mla_mixer(q, k, v, qc, kc, vc):
    B_, L = q.shape[:2]
    nb = L // MLA_BLOCK
    scale = (B_NOPE + B_ROPE) ** -0.5
    qb = jnp.moveaxis(q.reshape(B_, nb, MLA_BLOCK, B_HEADS, B_NOPE + B_ROPE), 1, 0)

    def block(q_i):
        s_lat = jnp.einsum('bqhd,bkhd->bhqk', q_i, k).astype(jnp.float32) * scale
        s_ctx = jnp.einsum('bqhd,bkhd->bhqk', q_i, kc).astype(jnp.float32) * scale
        p_lat, p_ctx = joint_softmax([s_lat, s_ctx])
        return (jnp.einsum('bhqk,bkhd->bqhd', p_lat.astype(v.dtype), v)
                + jnp.einsum('bhqk,bkhd->bqhd', p_ctx.astype(vc.dtype), vc))

    o = lax.map(block, qb)
    y = jnp.moveaxis(o, 0, 1).reshape(B_, L, B_HEADS * B_VDIM)
    yc = dense_attention(qc, kc, vc) if qc is not None else None
    return y, yc


def s5_discretize(lam_re, lam_im, log_step, b_re, b_im):
    lr, li = lam_re.astype(jnp.float32), lam_im.astype(jnp.float32)
    step = jnp.exp(log_step.astype(jnp.float32))[:, None]
    mag = jnp.exp(lr * step)
    abr, abi = mag * jnp.cos(li * step), mag * jnp.sin(li * step)
    nr, ni = abr - 1.0, abi
    den = lr * lr + li * li
    fr = (nr * lr + ni * li) / den
    fi = (ni * lr - nr * li) / den
    br, bi = b_re.astype(jnp.float32), b_im.astype(jnp.float32)
    bbr = fr[..., None] * br - fi[..., None] * bi
    bbi = fr[..., None] * bi + fi[..., None] * br
    return abr, abi, bbr, bbi


def _cmul_combine(e1, e2):
    a1r, a1i, b1r, b1i = e1
    a2r, a2i, b2r, b2i = e2
    return (a2r * a1r - a2i * a1i, a2r * a1i + a2i * a1r,
            a2r * b1r - a2i * b1i + b2r, a2r * b1i + a2i * b1r + b2i)


def s5_scan(u, abr, abi, bbr, bbi, h0r, h0i, reverse):
    bu_r = jnp.einsum('blgc,gpc->blgp', u, bbr)
    bu_i = jnp.einsum('blgc,gpc->blgp', u, bbi)
    first = -1 if reverse else 0
    bu_r = bu_r.at[:, first].add(abr * h0r - abi * h0i)
    bu_i = bu_i.at[:, first].add(abr * h0i + abi * h0r)
    a_r = jnp.broadcast_to(abr, bu_r.shape)
    a_i = jnp.broadcast_to(abi, bu_i.shape)
    _, _, h_r, h_i = lax.associative_scan(_cmul_combine, (a_r, a_i, bu_r, bu_i), reverse=reverse, axis=1)
    return h_r, h_i


def s5_readout(h_r, h_i, c_r, c_i):
    return jnp.einsum('blgp,gcp->blgc', h_r, c_r) - jnp.einsum('blgp,gcp->blgc', h_i, c_i)


def s5_glu(y, w_glu, dtype):
    z = jax.nn.gelu(y).astype(w_glu.dtype) @ w_glu
    a, g = jnp.split(z, 2, axis=-1)
    return (a * jax.nn.sigmoid(g)).astype(dtype)


def s5_mixer(u, uc, lam_re, lam_im, log_step, b_re, b_im, c_re, c_im, d_skip, w_glu, need_ctx_out):
    B_, L = u.shape[:2]
    Lc = uc.shape[1]
    d_g = d_skip.astype(jnp.float32).reshape(C_NGROUPS, C_GROUP)
    ug = u.astype(jnp.float32).reshape(B_, L, C_NGROUPS, C_GROUP)
    ucg = uc.astype(jnp.float32).reshape(B_, Lc, C_NGROUPS, C_GROUP)
    y = ug * d_g
    yc = ucg * d_g if need_ctx_out else None
    h0 = jnp.zeros((B_, C_NGROUPS, C_STATE), jnp.float32)
    for dirn, reverse in ((0, False), (1, True)):
        abr, abi, bbr, bbi = s5_discretize(lam_re[dirn], lam_im[dirn], log_step[dirn], b_re[dirn], b_im[dirn])
        c_r, c_i = c_re[dirn].astype(jnp.float32), c_im[dirn].astype(jnp.float32)
        hc_r, hc_i = s5_scan(ucg, abr, abi, bbr, bbi, h0, h0, reverse)
        last = 0 if reverse else -1
        h_r, h_i = s5_scan(ug, abr, abi, bbr, bbi, hc_r[:, last], hc_i[:, last], reverse)
        y = y + s5_readout(h_r, h_i, c_r, c_i)
        if need_ctx_out:
            yc = yc + s5_readout(hc_r, hc_i, c_r, c_i)
    y_out = s5_glu(y.reshape(B_, L, C_CH), w_glu, u.dtype)
    yc_out = s5_glu(yc.reshape(B_, Lc, C_CH), w_glu, uc.dtype) if need_ctx_out else None
    return y_out, yc_out


def natten_mixer(q, k, v, qc, kc, vc, rpb):
    B_, L = q.shape[:2]
    n_rows = L // GRID_W
    kh = min(NA_KH, n_rows)
    scale = HEAD_DIM ** -0.5
    qg = q.reshape(B_, n_rows, GRID_W, D_HEADS, HEAD_DIM)
    kg = k.reshape(B_, n_rows, GRID_W, D_HEADS, HEAD_DIM)
    vg = v.reshape(B_, n_rows, GRID_W, D_HEADS, HEAD_DIM)
    col = jnp.arange(GRID_W)
    col_idx = jnp.clip(col - NA_KW // 2, 0, GRID_W - NA_KW)[:, None] + jnp.arange(NA_KW)[None, :]
    rpb_cols = rpb.astype(jnp.float32)[:, :, col_idx - col[:, None] + (NA_KW - 1)]

    def row_block(r):
        r0 = jnp.clip(r - kh // 2, 0, n_rows - kh)
        k_sel = lax.dynamic_slice_in_dim(kg, r0, kh, axis=1)[:, :, col_idx]
        v_sel = lax.dynamic_slice_in_dim(vg, r0, kh, axis=1)[:, :, col_idx]
        q_r = lax.dynamic_index_in_dim(qg, r, axis=1, keepdims=False)
        s_loc = jnp.einsum('bwhd,bywjhd->bhwyj', q_r, k_sel).astype(jnp.float32) * scale
        bias = jnp.transpose(rpb_cols[:, r0 + jnp.arange(kh) - r + (NA_KH - 1)], (0, 2, 1, 3))
        s_loc = (s_loc + bias[None]).reshape(B_, D_HEADS, GRID_W, kh * NA_KW)
        s_ctx = jnp.einsum('bwhd,bkhd->bhwk', q_r, kc).astype(jnp.float32) * scale
        p_loc, p_ctx = joint_softmax([s_loc, s_ctx])
        p_loc = p_loc.reshape(B_, D_HEADS, GRID_W, kh, NA_KW).astype(v.dtype)
        return (jnp.einsum('bhwyj,bywjhd->bwhd', p_loc, v_sel)
                + jnp.einsum('bhwk,bkhd->bwhd', p_ctx.astype(vc.dtype), vc))

    o = lax.map(row_block, jnp.arange(n_rows))
    y = jnp.moveaxis(o, 0, 1).reshape(B_, L, D_HEADS * HEAD_DIM)
    yc = dense_attention(qc, kc, vc) if qc is not None else None
    return y, yc


def token_mixers(h, hc, row_pos, col_pos, p, need_ctx_out):
    aq, ak, av, bq, bkv, bkr, cu, dq, dk, dv = split_cols(h @ p['w_in'])
    aqc, akc, avc, bqc, bkvc, bkrc, cuc, dqc, dkc, dvc = split_cols(hc @ p['w_in'])

    def heads(t, n):
        return t.reshape(t.shape[0], t.shape[1], n, HEAD_DIM)

    y_a, y_ac = swa_mixer(axial_rope(heads(aq, A_HEADS), row_pos, col_pos),
                          axial_rope(heads(ak, A_KV_HEADS), row_pos, col_pos),
                          heads(av, A_KV_HEADS),
                          heads(aqc, A_HEADS) if need_ctx_out else None,
                          heads(akc, A_KV_HEADS), heads(avc, A_KV_HEADS), p['swa_sink'])

    q_b = mla_q(bq, p['mla_q_norm_g'], p['mla_w_uq'], row_pos, col_pos)
    k_b, v_b = mla_kv(bkv, bkr, p['mla_kv_norm_g'], p['mla_w_ukv'], row_pos, col_pos)
    k_bc, v_bc = mla_kv(bkvc, bkrc, p['mla_kv_norm_g'], p['mla_w_ukv'], None, None)
    q_bc = mla_q(bqc, p['mla_q_norm_g'], p['mla_w_uq'], None, None) if need_ctx_out else None
    y_b, y_bc = mla_mixer(q_b, k_b, v_b, q_bc, k_bc, v_bc)

    y_s, y_sc = s5_mixer(cu, cuc, p['s5_lambda_re'], p['s5_lambda_im'], p['s5_log_step'],
                         p['s5_b_re'], p['s5_b_im'], p['s5_c_re'], p['s5_c_im'], p['s5_d'],
                         p['s5_w_glu'], need_ctx_out)

    y_d, y_dc = natten_mixer(heads(dq, D_HEADS), heads(dk, D_HEADS), heads(dv, D_HEADS),
                             heads(dqc, D_HEADS) if need_ctx_out else None,
                             heads(dkc, D_HEADS), heads(dvc, D_HEADS), p['na_rpb'])

    y = group_rmsnorm(jnp.concatenate([y_a, y_b, y_s, y_d], axis=-1), p['mix_norm_g']) @ p['w_out']
    yc = None
    if need_ctx_out:
        yc = group_rmsnorm(jnp.concatenate([y_ac, y_bc, y_sc, y_dc], axis=-1), p['mix_norm_g']) @ p['w_out']
    return y, yc


def peer_ffn(h, w_q, sub_keys, u_tab, v_tab):
    shp = h.shape
    t = h.reshape(-1, shp[-1])
    n = t.shape[0]
    half = PEER_DKEY // 2
    q = (t @ w_q).reshape(n, PEER_HEADS, 2, half)
    s1 = jnp.einsum('nhd,kd->nhk', q[:, :, 0], sub_keys[0]).astype(jnp.float32)
    s2 = jnp.einsum('nhd,kd->nhk', q[:, :, 1], sub_keys[1]).astype(jnp.float32)
    v1, i1 = lax.top_k(s1, PEER_TOPK)
    v2, i2 = lax.top_k(s2, PEER_TOPK)
    cand_s = (v1[..., :, None] + v2[..., None, :]).reshape(n, PEER_HEADS, PEER_TOPK * PEER_TOPK)
    cand_i = (i1[..., :, None] * PEER_NKEYS + i2[..., None, :]).reshape(n, PEER_HEADS, PEER_TOPK * PEER_TOPK)
    top_s, pos = lax.top_k(cand_s, PEER_TOPK)
    idx = jnp.take_along_axis(cand_i, pos, axis=-1)
    gate = jax.nn.softmax(top_s, axis=-1)
    nc = n // PEER_CHUNK

    def expert_chunk(args):
        tc, ic, gc = args
        act = jax.nn.gelu(jnp.einsum('cd,chkd->chk', tc, u_tab[ic]))
        w = gc.astype(act.dtype) * act
        return jnp.einsum('chk,chkd->cd', w.astype(v_tab.dtype), v_tab[ic])

    o = lax.map(expert_chunk, (t.reshape(nc, PEER_CHUNK, shp[-1]),
                               idx.reshape(nc, PEER_CHUNK, PEER_HEADS, PEER_TOPK),
                               gate.reshape(nc, PEER_CHUNK, PEER_HEADS, PEER_TOPK)))
    return o.reshape(shp).astype(h.dtype)


def hybrid_layer(x, xc, c, c_ctx, row_pos, col_pos, p, update_ctx):
    mod = jax.nn.silu(c) @ p['w_ada'] + p['b_ada']
    sh1, sc1, g1, sh2, sc2, g2 = [m[:, None, :] for m in jnp.split(mod, 6, axis=-1)]
    modc = jax.nn.silu(c_ctx) @ p['w_ada'] + p['b_ada']
    sh1c, sc1c, g1c, sh2c, sc2c, g2c = jnp.split(modc, 6, axis=-1)
    h = modulate(x, p['norm1_g'], sh1, sc1)
    hc = modulate(xc, p['norm1_g'], sh1c, sc1c)
    y, yc = token_mixers(h, hc, row_pos, col_pos, p, update_ctx)
    x = x + g1 * y
    x = x + g2 * peer_ffn(modulate(x, p['norm2_g'], sh2, sc2), p['peer_w_q'], p['peer_sub_keys'], p['peer_u'], p['peer_v'])
    if update_ctx:
        xc = xc + g1c * yc
        xc = xc + g2c * peer_ffn(modulate(xc, p['norm2_g'], sh2c, sc2c), p['peer_w_q'], p['peer_sub_keys'], p['peer_u'], p['peer_v'])
    return x, xc


def setup_inputs(seed: int = 0) -> dict:
    key = jax.random.key(seed)
    ks = iter(jax.random.split(key, 40))

    def nrm(shape, s):
        return jax.random.normal(next(ks), shape, jnp.float32) * s

    G, P, CG = C_NGROUPS, C_STATE, C_GROUP
    x = nrm((BATCH, SEQ, D_MODEL), 1.0)
    c = nrm((BATCH, D_MODEL), 1.0)
    ctx = nrm((BATCH, CTX_LEN, D_MODEL), 1.0)
    c_ctx = nrm((D_MODEL,), 1.0)
    norm1_g = 1.0 + nrm((DEPTH, D_MODEL), 0.02)
    norm2_g = 1.0 + nrm((DEPTH, D_MODEL), 0.02)
    w_ada = nrm((DEPTH, D_MODEL, 6 * D_MODEL), 0.5 * D_MODEL ** -0.5)
    b_ada = nrm((DEPTH, 6 * D_MODEL), 0.02)
    w_in = nrm((DEPTH, D_MODEL, IN_WIDTH), D_MODEL ** -0.5)
    swa_sink = nrm((DEPTH, A_HEADS), 0.5)
    mla_q_norm_g = 1.0 + nrm((DEPTH, B_Q_LORA), 0.02)
    mla_w_uq = nrm((DEPTH, B_Q_LORA, B_HEADS * (B_NOPE + B_ROPE)), B_Q_LORA ** -0.5)
    mla_kv_norm_g = 1.0 + nrm((DEPTH, B_KV_LORA), 0.02)
    mla_w_ukv = nrm((DEPTH, B_KV_LORA, B_HEADS * (B_NOPE + B_VDIM)), B_KV_LORA ** -0.5)
    s5_lambda_re = -0.5 * jnp.exp(nrm((DEPTH, 2, G, P), 0.02))
    s5_lambda_im = math.pi * jnp.arange(P, dtype=jnp.float32) + nrm((DEPTH, 2, G, P), 0.01)
    s5_log_step = math.log(STEP_MIN) + jax.random.uniform(next(ks), (DEPTH, 2, G), jnp.float32) * (math.log(STEP_MAX) - math.log(STEP_MIN))
    s5_b_re = nrm((DEPTH, 2, G, P, CG), (2 * CG) ** -0.5)
    s5_b_im = nrm((DEPTH, 2, G, P, CG), (2 * CG) ** -0.5)
    s5_c_re = nrm((DEPTH, 2, G, CG, P), (2 * P) ** -0.5)
    s5_c_im = nrm((DEPTH, 2, G, CG, P), (2 * P) ** -0.5)
    s5_d = nrm((DEPTH, C_CH), 1.0)
    s5_w_glu = nrm((DEPTH, C_CH, 2 * C_CH), C_CH ** -0.5)
    na_rpb = nrm((DEPTH, D_HEADS, 2 * NA_KH - 1, 2 * NA_KW - 1), 0.1)
    mix_norm_g = 1.0 + nrm((DEPTH, D_MIX), 0.02)
    w_out = nrm((DEPTH, D_MIX, D_MODEL), D_MIX ** -0.5)
    peer_w_q = nrm((DEPTH, D_MODEL, PEER_HEADS * PEER_DKEY), D_MODEL ** -0.5)
    peer_sub_keys = nrm((DEPTH, 2, PEER_NKEYS, PEER_DKEY // 2), (PEER_DKEY // 2) ** -0.5)
    peer_u = nrm((DEPTH, PEER_NEXPERTS, D_MODEL), D_MODEL ** -0.5)
    peer_v = nrm((DEPTH, PEER_NEXPERTS, D_MODEL), 1.0)
    final_norm_g = 1.0 + nrm((D_MODEL,), 0.02)
    return {'x': x, 'c': c, 'ctx': ctx, 'c_ctx': c_ctx,
            'norm1_g': norm1_g, 'norm2_g': norm2_g, 'w_ada': w_ada, 'b_ada': b_ada, 'w_in': w_in,
            'swa_sink': swa_sink,
            'mla_q_norm_g': mla_q_norm_g, 'mla_w_uq': mla_w_uq, 'mla_kv_norm_g': mla_kv_norm_g, 'mla_w_ukv': mla_w_ukv,
            's5_lambda_re': s5_lambda_re, 's5_lambda_im': s5_lambda_im, 's5_log_step': s5_log_step,
            's5_b_re': s5_b_re, 's5_b_im': s5_b_im, 's5_c_re': s5_c_re, 's5_c_im': s5_c_im,
            's5_d': s5_d, 's5_w_glu': s5_w_glu,
            'na_rpb': na_rpb, 'mix_norm_g': mix_norm_g, 'w_out': w_out,
            'peer_w_q': peer_w_q, 'peer_sub_keys': peer_sub_keys, 'peer_u': peer_u, 'peer_v': peer_v,
            'final_norm_g': final_norm_g}


def reference(x, c, ctx, c_ctx, norm1_g, norm2_g, w_ada, b_ada, w_in, swa_sink,
              mla_q_norm_g, mla_w_uq, mla_kv_norm_g, mla_w_ukv,
              s5_lambda_re, s5_lambda_im, s5_log_step, s5_b_re, s5_b_im, s5_c_re, s5_c_im, s5_d, s5_w_glu,
              na_rpb, mix_norm_g, w_out, peer_w_q, peer_sub_keys, peer_u, peer_v, final_norm_g):
    L = x.shape[1]
    t = jnp.arange(L)
    row_pos = t // GRID_W
    col_pos = t % GRID_W
    xc = ctx
    for l in range(DEPTH):
        p = {'norm1_g': norm1_g[l], 'norm2_g': norm2_g[l], 'w_ada': w_ada[l], 'b_ada': b_ada[l],
             'w_in': w_in[l], 'swa_sink': swa_sink[l],
             'mla_q_norm_g': mla_q_norm_g[l], 'mla_w_uq': mla_w_uq[l],
             'mla_kv_norm_g': mla_kv_norm_g[l], 'mla_w_ukv': mla_w_ukv[l],
             's5_lambda_re': s5_lambda_re[l], 's5_lambda_im': s5_lambda_im[l], 's5_log_step': s5_log_step[l],
             's5_b_re': s5_b_re[l], 's5_b_im': s5_b_im[l], 's5_c_re': s5_c_re[l], 's5_c_im': s5_c_im[l],
             's5_d': s5_d[l], 's5_w_glu': s5_w_glu[l],
             'na_rpb': na_rpb[l], 'mix_norm_g': mix_norm_g[l], 'w_out': w_out[l],
             'peer_w_q': peer_w_q[l], 'peer_sub_keys': peer_sub_keys[l], 'peer_u': peer_u[l], 'peer_v': peer_v[l]}
        x, xc = hybrid_layer(x, xc, c, c_ctx, row_pos, col_pos, p, update_ctx=(l < DEPTH - 1))
    return rmsnorm(x, final_norm_g)
```

```python
import functools
import math

import numpy as np
import jax
import jax.numpy as jnp
from jax import lax
from jax.experimental import pallas as pl
from jax.experimental.pallas import tpu as pltpu

F32 = jnp.float32
BF16 = jnp.bfloat16

D_MODEL = 1024
HEAD_DIM = 64
GROUP_WIDTH = 256
GRID_W = 64
EPS = 1e-6
NEG_INF = -1e30
ROPE_BASE = 10000.0
N_HEADS = 4
A_KV_HEADS = 2
A_WINDOW = 128
A_BLOCK = 128
B_NOPE = 64
B_ROPE = 32
B_Q_LORA = 192
B_KV_LORA = 128
C_GROUP = 16
C_NGROUPS = 16
C_STATE = 64
NA_KH = 8
NA_KW = 16
PEER_HEADS = 8
PEER_TOPK = 16

LANES = 128
HEAD_SLOT = LANES
TOKEN_TILE = 256
VMEM_LIMIT = 56 * 1024 * 1024

N_STATE = C_NGROUPS * C_STATE


def _cparams(sem):
    return pltpu.CompilerParams(dimension_semantics=sem, vmem_limit_bytes=VMEM_LIMIT)


def _nt_dot(a, b):
    return lax.dot_general(a, b, (((1,), (1,)), ((), ())), preferred_element_type=F32)


def _dot(a, b):
    return jnp.dot(a, b, preferred_element_type=F32)


def _ada_kernel(c_ref, w_ref, b_ref, o_ref):
    c = c_ref[...]
    s = c * jax.nn.sigmoid(c)
    o_ref[0] = _dot(s.astype(BF16), w_ref[0].astype(BF16)) + b_ref[0]


def _ada_call(cc, w_ada, b_ada):
    depth, d, n6 = w_ada.shape
    tn = 1536
    return pl.pallas_call(
        _ada_kernel,
        grid=(depth, n6 // tn),
        in_specs=[pl.BlockSpec((cc.shape[0], d), lambda l, j: (0, 0)),
                  pl.BlockSpec((1, d, tn), lambda l, j: (l, 0, j)),
                  pl.BlockSpec((1, 1, tn), lambda l, j: (l, 0, j))],
        out_specs=pl.BlockSpec((1, cc.shape[0], tn), lambda l, j: (l, 0, j)),
        out_shape=jax.ShapeDtypeStruct((depth, cc.shape[0], n6), F32),
        compiler_params=_cparams(("arbitrary", "arbitrary")),
        name="ada_mod",
    )(cc, w_ada, b_ada.reshape(depth, 1, n6))


_C_AQ = 0
_C_AK = _C_AQ + 4 * HEAD_SLOT
_C_AV = _C_AK + 2 * HEAD_SLOT
_C_BQ = _C_AV + 2 * HEAD_SLOT
_C_BKV = _C_BQ + 256
_C_BKR = _C_BKV + 128
_C_CU = _C_BKR + 128
_C_DQ = _C_CU + 256
_C_DK = _C_DQ + 4 * HEAD_SLOT
_C_DV = _C_DK + 4 * HEAD_SLOT
_C_END = _C_DV + 4 * HEAD_SLOT


def _rms(x, n):
    return x * lax.rsqrt(jnp.sum(x * x, axis=-1, keepdims=True) * (1.0 / n) + EPS)


def _rope(t, tab_ref, shift):
    return (t * tab_ref[0]
            + pltpu.roll(t, shift, 1) * tab_ref[1]
            + pltpu.roll(t, LANES - shift, 1) * tab_ref[2])


def _proj_kernel(x_ref, mod_ref, g_ref, w_ref, ta_ref, tb_ref, qg_ref, wuq_ref, kvg_ref, wukv_ref,
                 qa_ref, ka_ref, va_ref, qb_ref, kb_ref, vb_ref, cu_ref, qd_ref, kd_ref, vd_ref):
    x = x_ref[...]
    m = mod_ref[0, 0]
    h = _rms(x, D_MODEL) * g_ref[...]
    h = h * (1.0 + m[1:2]) + m[0:1]
    z = _dot(h.astype(BF16), w_ref[...])

    a_scale = HEAD_DIM ** -0.5
    for hh in range(N_HEADS):
        c0 = _C_AQ + hh * HEAD_SLOT
        qa_ref[0, :, hh * HEAD_SLOT:(hh + 1) * HEAD_SLOT] = (
            _rope(z[:, c0:c0 + HEAD_SLOT], ta_ref, 16) * a_scale).astype(BF16)
    for g in range(A_KV_HEADS):
        c0 = _C_AK + g * HEAD_SLOT
        ka_ref[0, :, g * HEAD_SLOT:(g + 1) * HEAD_SLOT] = _rope(z[:, c0:c0 + HEAD_SLOT], ta_ref, 16).astype(BF16)
    va_ref[0] = z[:, _C_AV:_C_AV + 2 * HEAD_SLOT].astype(BF16)

    cq = _rms(z[:, _C_BQ:_C_BQ + 256], B_Q_LORA) * qg_ref[...]
    qb = _dot(cq.astype(BF16), wuq_ref[...])
    ckv = _rms(z[:, _C_BKV:_C_BKV + 128], B_KV_LORA) * kvg_ref[...]
    kv = _dot(ckv.astype(BF16), wukv_ref[...])
    kr = _rope(z[:, _C_BKR:_C_BKR + 128], tb_ref, 8)
    lane = lax.broadcasted_iota(jnp.int32, (1, HEAD_SLOT), 1)
    ones_col = (lane == HEAD_DIM).astype(F32)
    for hh in range(N_HEADS):
        sl = slice(hh * HEAD_SLOT, (hh + 1) * HEAD_SLOT)
        qb_ref[0, :, sl] = _rope(qb[:, sl], tb_ref, 8).astype(BF16)
        kb_ref[0, :, sl] = (kv[:, sl] + kr).astype(BF16)
        vsl = slice(4 * HEAD_SLOT + hh * HEAD_SLOT, 4 * HEAD_SLOT + (hh + 1) * HEAD_SLOT)
        vb_ref[0, :, sl] = (kv[:, vsl] + ones_col).astype(BF16)

    cu_ref[0] = z[:, _C_CU:_C_CU + 256]
    qd_ref[0] = (z[:, _C_DQ:_C_DQ + 4 * HEAD_SLOT] * a_scale).astype(BF16)
    kd_ref[0] = z[:, _C_DK:_C_DK + 4 * HEAD_SLOT].astype(BF16)
    vd_ref[0] = z[:, _C_DV:_C_DV + 4 * HEAD_SLOT].astype(BF16)


def _stream_block(b, t, n_b, lat_tiles):
    return jnp.where(t == 0, n_b * lat_tiles + b, b * lat_tiles + t - 1)


def _proj_call(x_all, modsel, g1, w_in_p, tab_a, tab_b, qg, wuq, kvg, wukv, n_b, s_len):
    tiles = s_len // TOKEN_TILE
    lat_tiles = tiles - 1
    T = TOKEN_TILE
    xmap = lambda b, t: (_stream_block(b, t, n_b, lat_tiles), 0)
    full = lambda shape: pl.BlockSpec(shape, lambda b, t: (0,) * len(shape))
    omap = lambda b, t: (b, t, 0)
    widths = [4 * HEAD_SLOT, 2 * HEAD_SLOT, 2 * HEAD_SLOT, 4 * HEAD_SLOT, 4 * HEAD_SLOT, 4 * HEAD_SLOT,
              256, 4 * HEAD_SLOT, 4 * HEAD_SLOT, 4 * HEAD_SLOT]
    dts = [BF16, BF16, BF16, BF16, BF16, BF16, F32, BF16, BF16, BF16]
    return pl.pallas_call(
        _proj_kernel,
        grid=(n_b, tiles),
        in_specs=[pl.BlockSpec((T, D_MODEL), xmap),
                  pl.BlockSpec((1, 1, 6, D_MODEL), lambda b, t: (b, jnp.minimum(t, 1), 0, 0)),
                  full((1, D_MODEL)),
                  full(w_in_p.shape),
                  pl.BlockSpec((3, T, LANES), lambda b, t: (0, t, 0)),
                  pl.BlockSpec((3, T, LANES), lambda b, t: (0, t, 0)),
                  full(qg.shape), full(wuq.shape), full(kvg.shape), full(wukv.shape)],
        out_specs=[pl.BlockSpec((1, T, w), omap) for w in widths],
        out_shape=[jax.ShapeDtypeStruct((n_b, s_len, w), dt) for w, dt in zip(widths, dts)],
        compiler_params=_cparams(("arbitrary", "arbitrary")),
        name="mod_proj",
    )(x_all, modsel, g1, w_in_p, tab_a, tab_b, qg, wuq, kvg, wukv)


def _swa_kernel(sink_ref, q_ref, k_ref, v_ref, o_ref, *, ctx_len, s_len):
    n = pl.program_id(1)
    is_lat = n >= ctx_len // A_BLOCK
    band = 3 * A_BLOCK
    ks = pl.multiple_of(jnp.clip((n - 1) * A_BLOCK, 0, s_len - band), A_BLOCK)
    q = q_ref[0]
    kl = k_ref[0, pl.ds(ks, band), :]
    vl = v_ref[0, pl.ds(ks, band), :]
    kc = k_ref[0, 0:ctx_len, :]
    vc = v_ref[0, 0:ctx_len, :]
    qpos = n * A_BLOCK + lax.broadcasted_iota(jnp.int32, (A_BLOCK, band), 0)
    kpos = ks + lax.broadcasted_iota(jnp.int32, (A_BLOCK, band), 1)
    window = jnp.where(is_lat, A_WINDOW, -1)
    valid = (jnp.abs(qpos - kpos) <= window) & (kpos >= ctx_len)
    rep = N_HEADS // A_KV_HEADS
    for hh in range(N_HEADS):
        g = hh // rep
        qs = slice(hh * HEAD_SLOT, (hh + 1) * HEAD_SLOT)
        gs = slice(g * HEAD_SLOT, (g + 1) * HEAD_SLOT)
        s_loc = jnp.where(valid, _nt_dot(q[:, qs], kl[:, gs]), NEG_INF)
        s_ctx = _nt_dot(q[:, qs], kc[:, gs])
        sk = sink_ref[hh]
        mx = jnp.maximum(jnp.max(s_loc, axis=-1, keepdims=True), jnp.max(s_ctx, axis=-1, keepdims=True))
        mx = jnp.maximum(mx, sk)
        p_loc = jnp.exp(s_loc - mx)
        p_ctx = jnp.exp(s_ctx - mx)
        den = (jnp.sum(p_loc, axis=-1, keepdims=True) + jnp.sum(p_ctx, axis=-1, keepdims=True)
               + jnp.exp(sk - mx))
        o = _dot(p_loc.astype(BF16), vl[:, gs]) + _dot(p_ctx.astype(BF16), vc[:, gs])
        o_ref[0, :, qs] = o * (1.0 / den)


def _swa_call(sink, qa, ka, va, ctx_len):
    n_b, s_len, _ = qa.shape
    kern = functools.partial(_swa_kernel, ctx_len=ctx_len, s_len=s_len)
    return pl.pallas_call(
        kern,
        grid=(n_b, s_len // A_BLOCK),
        in_specs=[pl.BlockSpec(memory_space=pltpu.SMEM),
                  pl.BlockSpec((1, A_BLOCK, 4 * HEAD_SLOT), lambda b, n: (b, n, 0)),
                  pl.BlockSpec((1, s_len, 2 * HEAD_SLOT), lambda b, n: (b, 0, 0)),
                  pl.BlockSpec((1, s_len, 2 * HEAD_SLOT), lambda b, n: (b, 0, 0))],
        out_specs=pl.BlockSpec((1, A_BLOCK, 4 * HEAD_SLOT), lambda b, n: (b, n, 0)),
        out_shape=jax.ShapeDtypeStruct((n_b, s_len, 4 * HEAD_SLOT), F32),
        compiler_params=_cparams(("arbitrary", "arbitrary")),
        name="swa_mixer",
    )(sink, qa, ka, va)


MLA_TQ = 256
MLA_CK = 512


def _mla_kernel(q_ref, k_ref, v_ref, o_ref, m_scr, acc_scr, *, ctx_len, s_len):
    t = pl.program_id(1)
    is_lat = t >= ctx_len // MLA_TQ
    scale = (B_NOPE + B_ROPE) ** -0.5
    n_chunks = (s_len - ctx_len) // MLA_CK
    lane = lax.broadcasted_iota(jnp.int32, (1, HEAD_SLOT), 1)

    for hh in range(N_HEADS):
        sl = slice(hh * HEAD_SLOT, (hh + 1) * HEAD_SLOT)
        q = q_ref[0, :, sl]

        def chunk(start, size, first):
            k = k_ref[0, pl.ds(start, size), sl]
            v = v_ref[0, pl.ds(start, size), sl]
            s = _nt_dot(q, k) * scale
            m_cur = jnp.max(s, axis=-1, keepdims=True)
            if first:
                m_new = jnp.broadcast_to(m_cur, (MLA_TQ, LANES))
                p = jnp.exp(s - jnp.concatenate([m_new] * (size // LANES), axis=1))
                acc_scr[...] = _dot(p.astype(BF16), v)
            else:
                m_prev = m_scr[...]
                m_new = jnp.maximum(m_prev, m_cur)
                alpha = jnp.exp(m_prev - m_new)
                p = jnp.exp(s - jnp.concatenate([m_new] * (size // LANES), axis=1))
                acc_scr[...] = acc_scr[...] * alpha + _dot(p.astype(BF16), v)
            m_scr[...] = m_new

        chunk(0, ctx_len, True)

        @pl.when(is_lat)
        def _():
            def body(c, carry):
                chunk(pl.multiple_of(ctx_len + c * MLA_CK, MLA_CK // 2), MLA_CK, False)
                return carry
            lax.fori_loop(0, n_chunks, body, 0)

        acc = acc_scr[...]
        den = acc[:, HEAD_DIM:HEAD_DIM + 1]
        o_ref[0, :, sl] = jnp.where(lane < HEAD_DIM, acc * (1.0 / den), 0.0)


def _mla_call(qb, kb, vb, ctx_len):
    n_b, s_len, _ = qb.shape
    kern = functools.partial(_mla_kernel, ctx_len=ctx_len, s_len=s_len)
    return pl.pallas_call(
        kern,
        grid=(n_b, s_len // MLA_TQ),
        in_specs=[pl.BlockSpec((1, MLA_TQ, 4 * HEAD_SLOT), lambda b, t: (b, t, 0)),
                  pl.BlockSpec((1, s_len, 4 * HEAD_SLOT), lambda b, t: (b, 0, 0)),
                  pl.BlockSpec((1, s_len, 4 * HEAD_SLOT), lambda b, t: (b, 0, 0))],
        out_specs=pl.BlockSpec((1, MLA_TQ, 4 * HEAD_SLOT), lambda b, t: (b, t, 0)),
        out_shape=jax.ShapeDtypeStruct((n_b, s_len, 4 * HEAD_SLOT), F32),
        scratch_shapes=[pltpu.VMEM((MLA_TQ, LANES), F32), pltpu.VMEM((MLA_TQ, HEAD_SLOT), F32)],
        compiler_params=_cparams(("arbitrary", "arbitrary")),
        name="mla_mixer",
    )(qb, kb, vb)


S5_T = 128


def _s5_kernel(u_ref, bm_ref, cm_ref, ar_ref, ai_ref, y_ref, hr_scr, hi_scr, bu_scr, *, n_b):
    d = pl.program_id(0)
    c = pl.program_id(1)

    @pl.when(c == 0)
    def _():
        hr_scr[...] = jnp.zeros_like(hr_scr)
        hi_scr[...] = jnp.zeros_like(hi_scr)

    u = u_ref[...].reshape(S5_T * n_b, GROUP_WIDTH)
    bu_scr[...] = _dot(u.astype(BF16), bm_ref[0])
    a_r = jnp.broadcast_to(ar_ref[0], (n_b, N_STATE))
    a_i = jnp.broadcast_to(ai_ref[0], (n_b, N_STATE))

    def step(i, carry):
        h_r, h_i = carry
        t = jnp.where(d == 0, i, S5_T - 1 - i)
        row = pl.multiple_of(t * n_b, n_b)
        b_r = bu_scr[pl.ds(row, n_b), 0:N_STATE]
        b_i = bu_scr[pl.ds(row, n_b), N_STATE:2 * N_STATE]
        n_r = a_r * h_r - a_i * h_i + b_r
        n_i = a_r * h_i + a_i * h_r + b_i
        bu_scr[pl.ds(row, n_b), 0:N_STATE] = n_r
        bu_scr[pl.ds(row, n_b), N_STATE:2 * N_STATE] = n_i
        return n_r, n_i

    h_r, h_i = lax.fori_loop(0, S5_T, step, (hr_scr[...], hi_scr[...]), unroll=4)
    hr_scr[...] = h_r
    hi_scr[...] = h_i
    y = _dot(bu_scr[...].astype(BF16), cm_ref[0])
    y_ref[0] = y.reshape(S5_T, n_b, GROUP_WIDTH)


def _s5_chunk(d, c, n_chunks, ctx_chunks):
    rev = jnp.where(c < ctx_chunks, ctx_chunks - 1 - c, n_chunks - 1 - (c - ctx_chunks))
    return jnp.where(d == 0, c, rev)


def _s5_call(cu_t, bmat, cmat, a_r, a_i, ctx_len):
    s_len, n_b, _ = cu_t.shape
    n_chunks = s_len // S5_T
    ctx_chunks = ctx_len // S5_T
    cmap = lambda d, c: (_s5_chunk(d, c, n_chunks, ctx_chunks), 0, 0)
    kern = functools.partial(_s5_kernel, n_b=n_b)
    return pl.pallas_call(
        kern,
        grid=(2, n_chunks),
        in_specs=[pl.BlockSpec((S5_T, n_b, GROUP_WIDTH), cmap),
                  pl.BlockSpec((1, GROUP_WIDTH, 2 * N_STATE), lambda d, c: (d, 0, 0)),
                  pl.BlockSpec((1, 2 * N_STATE, GROUP_WIDTH), lambda d, c: (d, 0, 0)),
                  pl.BlockSpec((1, 1, N_STATE), lambda d, c: (d, 0, 0)),
                  pl.BlockSpec((1, 1, N_STATE), lambda d, c: (d, 0, 0))],
        out_specs=pl.BlockSpec((1, S5_T, n_b, GROUP_WIDTH),
                               lambda d, c: (d, _s5_chunk(d, c, n_chunks, ctx_chunks), 0, 0)),
        out_shape=jax.ShapeDtypeStruct((2, s_len, n_b, GROUP_WIDTH), F32),
        scratch_shapes=[pltpu.VMEM((n_b, N_STATE), F32), pltpu.VMEM((n_b, N_STATE), F32),
                        pltpu.VMEM((S5_T * n_b, 2 * N_STATE), F32)],
        compiler_params=_cparams(("arbitrary", "arbitrary")),
        name="s5_scan",
    )(cu_t, bmat, cmat, a_r, a_i)


def _s5_glu_kernel(u_ref, y_ref, d_ref, w_ref, o_ref):
    y = u_ref[...] * d_ref[...] + y_ref[0] + y_ref[1]
    z = _dot(jax.nn.gelu(y).astype(BF16), w_ref[...])
    o_ref[...] = z[:, :GROUP_WIDTH] * jax.nn.sigmoid(z[:, GROUP_WIDTH:])


def _s5_glu_call(cu_flat, y_dirs, d_skip, w_glu):
    n = cu_flat.shape[0]
    tr = 1024
    return pl.pallas_call(
        _s5_glu_kernel,
        grid=(n // tr,),
        in_specs=[pl.BlockSpec((tr, GROUP_WIDTH), lambda i: (i, 0)),
                  pl.BlockSpec((2, tr, GROUP_WIDTH), lambda i: (0, i, 0)),
                  pl.BlockSpec((1, GROUP_WIDTH), lambda i: (0, 0)),
                  pl.BlockSpec((GROUP_WIDTH, 2 * GROUP_WIDTH), lambda i: (0, 0))],
        out_specs=pl.BlockSpec((tr, GROUP_WIDTH), lambda i: (i, 0)),
        out_shape=jax.ShapeDtypeStruct((n, GROUP_WIDTH), F32),
        compiler_params=_cparams(("arbitrary",)),
        name="s5_glu",
    )(cu_flat, y_dirs, d_skip, w_glu)


NA_KEYS = NA_KH * GRID_W


def _na_kernel(q_ref, k_ref, v_ref, bias_ref, o_ref, *, ctx_len, n_rows):
    j = pl.program_id(1)
    ctx_steps = ctx_len // GRID_W
    r0 = jnp.clip(j - ctx_steps - NA_KH // 2, 0, n_rows - NA_KH)
    ks = pl.multiple_of(ctx_len + r0 * GRID_W, GRID_W)
    q = q_ref[0]
    kl = k_ref[0, pl.ds(ks, NA_KEYS), :]
    vl = v_ref[0, pl.ds(ks, NA_KEYS), :]
    kc = k_ref[0, 0:ctx_len, :]
    vc = v_ref[0, 0:ctx_len, :]
    for hh in range(N_HEADS):
        sl = slice(hh * HEAD_SLOT, (hh + 1) * HEAD_SLOT)
        s_loc = _nt_dot(q[:, sl], kl[:, sl]) + bias_ref[0, hh]
        s_ctx = _nt_dot(q[:, sl], kc[:, sl])
        mx = jnp.maximum(jnp.max(s_loc, axis=-1, keepdims=True), jnp.max(s_ctx, axis=-1, keepdims=True))
        p_loc = jnp.exp(s_loc - mx)
        p_ctx = jnp.exp(s_ctx - mx)
        den = jnp.sum(p_loc, axis=-1, keepdims=True) + jnp.sum(p_ctx, axis=-1, keepdims=True)
        o = _dot(p_loc.astype(BF16), vl[:, sl]) + _dot(p_ctx.astype(BF16), vc[:, sl])
        o_ref[0, :, sl] = o * (1.0 / den)


def _na_call(qd, kd, vd, bias, ctx_len):
    n_b, s_len, _ = qd.shape
    n_rows = (s_len - ctx_len) // GRID_W
    ctx_steps = ctx_len // GRID_W
    kern = functools.partial(_na_kernel, ctx_len=ctx_len, n_rows=n_rows)

    def bias_map(b, j):
        r = j - ctx_steps
        var = r - jnp.clip(r - NA_KH // 2, 0, n_rows - NA_KH)
        return (jnp.where(j < ctx_steps, NA_KH, var), 0, 0, 0)

    return pl.pallas_call(
        kern,
        grid=(n_b, ctx_steps + n_rows),
        in_specs=[pl.BlockSpec((1, GRID_W, 4 * HEAD_SLOT), lambda b, j: (b, j, 0)),
                  pl.BlockSpec((1, s_len, 4 * HEAD_SLOT), lambda b, j: (b, 0, 0)),
                  pl.BlockSpec((1, s_len, 4 * HEAD_SLOT), lambda b, j: (b, 0, 0)),
                  pl.BlockSpec((1, N_HEADS, GRID_W, NA_KEYS), bias_map)],
        out_specs=pl.BlockSpec((1, GRID_W, 4 * HEAD_SLOT), lambda b, j: (b, j, 0)),
        out_shape=jax.ShapeDtypeStruct((n_b, s_len, 4 * HEAD_SLOT), F32),
        compiler_params=_cparams(("arbitrary", "arbitrary")),
        name="na_mixer",
    )(qd, kd, vd, bias)


def _out_kernel(ya_ref, yb_ref, ys_ref, yd_ref, x_ref, mod_ref, mg_ref, w_ref, g2_ref, x1_ref, h2_ref):
    m = mod_ref[0, 0]
    mg = mg_ref[...]
    parts = []
    off = 0
    for ref, width in ((ya_ref, 4 * HEAD_SLOT), (yb_ref, 4 * HEAD_SLOT), (ys_ref, GROUP_WIDTH),
                       (yd_ref, 4 * HEAD_SLOT)):
        parts.append((_rms(ref[0], GROUP_WIDTH) * mg[:, off:off + width]).astype(BF16))
        off += width
    y = _dot(jnp.concatenate(parts, axis=-1), w_ref[...])
    x1 = x_ref[...] + m[2:3] * y
    x1_ref[...] = x1
    h2 = _rms(x1, D_MODEL) * g2_ref[...]
    h2_ref[...] = (h2 * (1.0 + m[4:5]) + m[3:4]).astype(BF16)


def _out_call(ya, yb, ys, yd, x_all, modsel, mixg_p, w_out_p, g2):
    n_b, s_len, _ = ya.shape
    tiles = s_len // TOKEN_TILE
    lat_tiles = tiles - 1
    T = TOKEN_TILE
    xmap = lambda b, t: (_stream_block(b, t, n_b, lat_tiles), 0)
    full = lambda shape: pl.BlockSpec(shape, lambda b, t: (0,) * len(shape))
    ymap = lambda b, t: (b, t, 0)
    return pl.pallas_call(
        _out_kernel,
        grid=(n_b, tiles),
        in_specs=[pl.BlockSpec((1, T, ya.shape[-1]), ymap), pl.BlockSpec((1, T, yb.shape[-1]), ymap),
                  pl.BlockSpec((1, T, ys.shape[-1]), ymap), pl.BlockSpec((1, T, yd.shape[-1]), ymap),
                  pl.BlockSpec((T, D_MODEL), xmap),
                  pl.BlockSpec((1, 1, 6, D_MODEL), lambda b, t: (b, jnp.minimum(t, 1), 0, 0)),
                  full(mixg_p.shape), full(w_out_p.shape), full((1, D_MODEL))],
        out_specs=[pl.BlockSpec((T, D_MODEL), xmap), pl.BlockSpec((T, D_MODEL), xmap)],
        out_shape=[jax.ShapeDtypeStruct(x_all.shape, F32), jax.ShapeDtypeStruct(x_all.shape, BF16)],
        compiler_params=_cparams(("arbitrary", "arbitrary")),
        name="out_proj",
    )(ya, yb, ys, yd, x_all, modsel, mixg_p, w_out_p, g2)


PEER_TT = 512
PEER_EB = 1024


def _peer_kernel(h2_ref, x1_ref, mod_ref, wq_ref, sk_ref, u_ref, vt_ref, fg_ref, o_ref,
                 h2t_scr, q_scr, s1_scr, c1_scr, s2_scr, e2_scr, tau_scr, v1_scr, v2_scr,
                 z_scr, p_scr, out_scr, *, n_keys, final):
    e = pl.program_id(1)
    n_e = pl.num_programs(1)
    TT = PEER_TT
    neg = -jnp.inf

    @pl.when(e == 0)
    def _():
        h2 = h2_ref[...]
        h2t_scr[...] = h2.astype(F32).T.astype(BF16)
        q_scr[...] = _dot(h2, wq_ref[...]).astype(BF16)
        out_scr[...] = jnp.zeros_like(out_scr)

        def top16(s, v_scr):
            w = s
            for k in range(PEER_TOPK):
                mk = jnp.max(w, axis=0, keepdims=True)
                v_scr[k:k + 1, :] = mk
                if k + 1 < PEER_TOPK:
                    w = jnp.where(w == mk, neg, w)

        def head(hh, carry):
            c0 = pl.multiple_of(hh * 256, 256)
            s1 = _nt_dot(sk_ref[0], q_scr[:, pl.ds(c0, 128)])
            s2 = _nt_dot(sk_ref[1], q_scr[:, pl.ds(c0 + 128, 128)])
            top16(s1, v1_scr)
            top16(s2, v2_scr)
            slabs = [v1_scr[0:1, :] + v2_scr[0:16, :]]
            for a in range(1, 5):
                slabs.append(v1_scr[a:a + 1, :] + v2_scr[0:8, :])
            slabs.append(v1_scr[0:8, :] + v2_scr[1:2, :])
            slabs.append(v1_scr[0:8, :] + v2_scr[0:1, :])
            slabs.append(v1_scr[8:16, :] + v2_scr[0:1, :])
            top = v1_scr[0:1, :] + v2_scr[0:1, :]
            zsum = jnp.zeros((1, TT), F32)
            mk = top
            for k in range(PEER_TOPK):
                mk = slabs[0].max(axis=0, keepdims=True)
                for sl in slabs[1:]:
                    mk = jnp.maximum(mk, sl.max(axis=0, keepdims=True))
                zsum = zsum + jnp.exp(mk - top)
                if k + 1 < PEER_TOPK:
                    slabs = [jnp.where(sl == mk, neg, sl) for sl in slabs]
            s1_scr[hh] = s1
            s2_scr[hh] = s2
            c1_scr[hh] = jnp.exp(s1 - v1_scr[0:1, :]) * (1.0 / zsum)
            e2_scr[hh] = jnp.exp(s2 - v2_scr[0:1, :])
            tau_scr[hh] = jnp.broadcast_to(mk, (8, TT))
            return carry

        lax.fori_loop(0, PEER_HEADS, head, 0)

    z_scr[...] = _dot(u_ref[...], h2t_scr[...])
    i_per = u_ref.shape[0] // n_keys
    jt_n = n_keys // 16

    def lane_tile(lt, carry):
        l0 = pl.multiple_of(lt * LANES, LANES)
        tr = [tau_scr[hh, 0:1, pl.ds(l0, LANES)] for hh in range(PEER_HEADS)]
        for ii in range(i_per):
            i8 = pl.multiple_of(e * i_per + (ii // 8) * 8, 8)
            r8 = slice(ii % 8, ii % 8 + 1)
            s1r = [s1_scr[hh, pl.ds(i8, 8), pl.ds(l0, LANES)][r8] for hh in range(PEER_HEADS)]
            c1r = [c1_scr[hh, pl.ds(i8, 8), pl.ds(l0, LANES)][r8] for hh in range(PEER_HEADS)]
            for jt in range(jt_n):
                rows = slice(jt * 16, (jt + 1) * 16)
                w = jnp.zeros((16, LANES), F32)
                for hh in range(PEER_HEADS):
                    sm = s2_scr[hh, rows, pl.ds(l0, LANES)] + s1r[hh]
                    w = w + jnp.where(sm >= tr[hh], e2_scr[hh, rows, pl.ds(l0, LANES)] * c1r[hh], 0.0)
                zr = slice(ii * n_keys + jt * 16, ii * n_keys + (jt + 1) * 16)
                act = jax.nn.gelu(z_scr[zr, pl.ds(l0, LANES)])
                p_scr[zr, pl.ds(l0, LANES)] = (w * act).astype(BF16)
        return carry

    lax.fori_loop(0, TT // LANES, lane_tile, 0)
    out_scr[...] += _dot(vt_ref[...], p_scr[...])

    @pl.when(e == n_e - 1)
    def _():
        x2 = x1_ref[...] + mod_ref[0, 0][5:6] * out_scr[...].T
        if final:
            x2 = _rms(x2, D_MODEL) * fg_ref[...]
        o_ref[...] = x2


def _peer_call(h2_all, x1_all, modsel, wq, subk, u_tab, vt_tab, fg, n_tok, n_lat, l_len, final):
    n_exp = u_tab.shape[0]
    n_keys = subk.shape[1]
    assert n_exp == n_keys * n_keys and n_keys % 16 == 0
    assert min(PEER_EB, n_exp) % (8 * n_keys) == 0
    assert n_tok % PEER_TT == 0 and n_lat % PEER_TT == 0 and l_len % PEER_TT == 0
    TT, EB = PEER_TT, min(PEER_EB, n_exp)
    lat_tiles = n_lat // TT
    per_b = l_len // TT

    def mod_map(i, e):
        return (jnp.where(i < lat_tiles, i // per_b, 0), (i < lat_tiles).astype(jnp.int32), 0, 0)

    kern = functools.partial(_peer_kernel, n_keys=n_keys, final=final)
    full = lambda shape: pl.BlockSpec(shape, lambda i, e: (0,) * len(shape))
    return pl.pallas_call(
        kern,
        grid=(n_tok // TT, n_exp // EB),
        in_specs=[pl.BlockSpec((TT, D_MODEL), lambda i, e: (i, 0)),
                  pl.BlockSpec((TT, D_MODEL), lambda i, e: (i, 0)),
                  pl.BlockSpec((1, 1, 6, D_MODEL), mod_map),
                  full(wq.shape), full(subk.shape),
                  pl.BlockSpec((EB, D_MODEL), lambda i, e: (e, 0)),
                  pl.BlockSpec((D_MODEL, EB), lambda i, e: (0, e)),
                  full((1, D_MODEL))],
        out_specs=pl.BlockSpec((TT, D_MODEL), lambda i, e: (i, 0)),
        out_shape=jax.ShapeDtypeStruct((n_tok, D_MODEL), F32),
        scratch_shapes=[pltpu.VMEM((D_MODEL, TT), BF16),
                        pltpu.VMEM((TT, PEER_HEADS * 256), BF16),
                        pltpu.VMEM((PEER_HEADS, n_keys, TT), F32),
                        pltpu.VMEM((PEER_HEADS, n_keys, TT), F32),
                        pltpu.VMEM((PEER_HEADS, n_keys, TT), F32),
                        pltpu.VMEM((PEER_HEADS, n_keys, TT), F32),
                        pltpu.VMEM((PEER_HEADS, 8, TT), F32),
                        pltpu.VMEM((PEER_TOPK, TT), F32),
                        pltpu.VMEM((PEER_TOPK, TT), F32),
                        pltpu.VMEM((EB, TT), F32),
                        pltpu.VMEM((EB, TT), BF16),
                        pltpu.VMEM((D_MODEL, TT), F32)],
        compiler_params=_cparams(("arbitrary", "arbitrary")),
        name="peer_ffn",
    )(h2_all, x1_all, modsel, wq, subk, u_tab, vt_tab, fg)


def _head_slots(w, n_heads, width=HEAD_DIM):
    lead = w.shape[:-1]
    w = w.reshape(lead + (n_heads, width))
    w = jnp.pad(w, [(0, 0)] * len(lead) + [(0, 0), (0, HEAD_SLOT - width)])
    return w.reshape(lead + (n_heads * HEAD_SLOT,))


def _prep_w_in(w_in):
    sizes = (256, 128, 128, B_Q_LORA, B_KV_LORA, B_ROPE, 256, 256, 256, 256)
    offs = np.cumsum((0,) + sizes)
    aq, ak, av, bq, bkv, bkr, cu, dq, dk, dv = [w_in[:, offs[i]:offs[i + 1]] for i in range(10)]
    d = w_in.shape[0]
    bkr_p = jnp.concatenate([jnp.zeros((d, B_NOPE), F32), bkr, jnp.zeros((d, HEAD_SLOT - B_NOPE - B_ROPE), F32)], 1)
    cols = [_head_slots(aq, 4), _head_slots(ak, 2), _head_slots(av, 2),
            jnp.pad(bq, ((0, 0), (0, 256 - B_Q_LORA))), bkv, bkr_p, cu,
            _head_slots(dq, 4), _head_slots(dk, 4), _head_slots(dv, 4)]
    w = jnp.concatenate(cols, axis=1)
    assert w.shape[1] == _C_END
    return w.astype(BF16)


def _prep_mla(w_uq, w_ukv, qg, kvg):
    wq = _head_slots(w_uq, N_HEADS, B_NOPE + B_ROPE)
    wq = jnp.pad(wq, ((0, 256 - B_Q_LORA), (0, 0))).astype(BF16)
    kv = w_ukv.reshape(B_KV_LORA, N_HEADS, B_NOPE + HEAD_DIM)
    wk = _head_slots(kv[:, :, :B_NOPE].reshape(B_KV_LORA, -1), N_HEADS, B_NOPE)
    wv = _head_slots(kv[:, :, B_NOPE:].reshape(B_KV_LORA, -1), N_HEADS, HEAD_DIM)
    wkv = jnp.concatenate([wk, wv], axis=1).astype(BF16)
    qg_p = jnp.pad(qg, (0, 256 - B_Q_LORA)).reshape(1, 256)
    return wq, wkv, qg_p, kvg.reshape(1, B_KV_LORA)


def _rope_tables(ctx_len, l_len):
    t = np.arange(l_len)
    pos = np.stack([t // GRID_W, t % GRID_W], 0).astype(np.float64)

    def build(width, lane0):
        a = width // 2
        half = a // 2
        inv = ROPE_BASE ** (-np.arange(half, dtype=np.float64) / half)
        cos = np.ones((l_len, LANES)); sp = np.zeros((l_len, LANES)); sm = np.zeros((l_len, LANES))
        for j in range(width):
            axis, i = j // a, j % a
            f, second = i % half, i >= half
            ang = pos[axis] * inv[f]
            cos[:, lane0 + j] = np.cos(ang)
            if second:
                sp[:, lane0 + j] = np.sin(ang)
            else:
                sm[:, lane0 + j] = -np.sin(ang)
        tab = np.stack([cos, sp, sm], 0)
        ctx = np.stack([np.ones((ctx_len, LANES)), np.zeros((ctx_len, LANES)), np.zeros((ctx_len, LANES))], 0)
        return jnp.asarray(np.concatenate([ctx, tab], axis=1), F32)

    return build(HEAD_DIM, 0), build(B_ROPE, B_NOPE)


def _s5_matrices(lam_re, lam_im, log_step, b_re, b_im, c_re, c_im):
    lr, li = lam_re.astype(F32), lam_im.astype(F32)
    step = jnp.exp(log_step.astype(F32))[..., None]
    mag = jnp.exp(lr * step)
    abr, abi = mag * jnp.cos(li * step), mag * jnp.sin(li * step)
    nr, ni = abr - 1.0, abi
    den = lr * lr + li * li
    fr = (nr * lr + ni * li) / den
    fi = (ni * lr - nr * li) / den
    bbr = fr[..., None] * b_re - fi[..., None] * b_im
    bbi = fr[..., None] * b_im + fi[..., None] * b_re
    eye = jnp.eye(C_NGROUPS, dtype=F32)
    bm_r = jnp.einsum('dgpc,gh->dgchp', bbr, eye).reshape(2, GROUP_WIDTH, N_STATE)
    bm_i = jnp.einsum('dgpc,gh->dgchp', bbi, eye).reshape(2, GROUP_WIDTH, N_STATE)
    bmat = jnp.concatenate([bm_r, bm_i], axis=2).astype(BF16)
    cm_r = jnp.einsum('dgcp,gh->dgphc', c_re.astype(F32), eye).reshape(2, N_STATE, GROUP_WIDTH)
    cm_i = jnp.einsum('dgcp,gh->dgphc', c_im.astype(F32), eye).reshape(2, N_STATE, GROUP_WIDTH)
    cmat = jnp.concatenate([cm_r, -cm_i], axis=1).astype(BF16)
    return bmat, cmat, abr.reshape(2, 1, N_STATE), abi.reshape(2, 1, N_STATE)


def _na_bias(rpb):
    w = np.arange(GRID_W)
    cs = np.clip(w - NA_KW // 2, 0, GRID_W - NA_KW)
    c = np.arange(GRID_W)
    inwin = (c[None, :] >= cs[:, None]) & (c[None, :] < cs[:, None] + NA_KW)
    colidx = np.clip(c[None, :] - w[:, None] + NA_KW - 1, 0, 2 * NA_KW - 2)
    y = np.arange(NA_KH)
    variants = []
    for d in range(NA_KH):
        rowidx = y - d + NA_KH - 1
        b = rpb.astype(F32)[:, rowidx][:, :, colidx]
        b = jnp.where(jnp.asarray(inwin)[None, None], b, NEG_INF)
        variants.append(jnp.transpose(b, (0, 2, 1, 3)).reshape(N_HEADS, GRID_W, NA_KEYS))
    variants.append(jnp.full((N_HEADS, GRID_W, NA_KEYS), NEG_INF, F32))
    return jnp.stack(variants, 0)


def _mix_layout(mix_norm_g, w_out):
    g = mix_norm_g.reshape(4, GROUP_WIDTH)
    w = w_out.reshape(4, GROUP_WIDTH, D_MODEL)
    gs, ws = [], []
    for k in range(4):
        if k == 2:
            gs.append(g[k]); ws.append(w[k])
        else:
            gs.append(_head_slots(g[k], N_HEADS))
            wk = w[k].reshape(N_HEADS, HEAD_DIM, D_MODEL)
            wk = jnp.pad(wk, ((0, 0), (0, HEAD_SLOT - HEAD_DIM), (0, 0))).reshape(N_HEADS * HEAD_SLOT, D_MODEL)
            ws.append(wk)
    return jnp.concatenate(gs).reshape(1, -1), jnp.concatenate(ws, axis=0).astype(BF16)


def kernel(x, c, ctx, c_ctx, norm1_g, norm2_g, w_ada, b_ada, w_in, swa_sink, mla_q_norm_g, mla_w_uq,
           mla_kv_norm_g, mla_w_ukv, s5_lambda_re, s5_lambda_im, s5_log_step, s5_b_re, s5_b_im, s5_c_re,
           s5_c_im, s5_d, s5_w_glu, na_rpb, mix_norm_g, w_out, peer_w_q, peer_sub_keys, peer_u, peer_v,
           final_norm_g):
    n_b, l_len, d = x.shape
    ctx_len = ctx.shape[1]
    depth = w_in.shape[0]
    s_len = ctx_len + l_len
    assert d == D_MODEL and ctx_len == TOKEN_TILE and n_b % 8 == 0 and n_b < 16
    assert l_len % MLA_CK == 0 and l_len // GRID_W >= NA_KH and l_len % PEER_TT == 0
    n_lat = n_b * l_len

    cc = jnp.concatenate([c, c_ctx[None], jnp.zeros((16 - n_b - 1, d), F32)], axis=0)
    mod = _ada_call(cc, w_ada, b_ada).reshape(depth, 16, 6, d)
    lat_mod = mod[:, :n_b]
    ctx_mod = jnp.broadcast_to(mod[:, n_b:n_b + 1], lat_mod.shape)
    modsel = jnp.stack([ctx_mod, lat_mod], axis=2)

    tab_a, tab_b = _rope_tables(ctx_len, l_len)
    x_all = jnp.concatenate([x.reshape(n_lat, d), ctx.reshape(n_b * ctx_len, d)], axis=0)
    fg = final_norm_g.reshape(1, d)

    for l in range(depth):
        last = l == depth - 1
        w_in_p = _prep_w_in(w_in[l])
        wuq, wukv, qg, kvg = _prep_mla(mla_w_uq[l], mla_w_ukv[l], mla_q_norm_g[l], mla_kv_norm_g[l])
        qa, ka, va, qb, kb, vb, cu, qd, kd, vd = _proj_call(
            x_all, modsel[l], norm1_g[l].reshape(1, d), w_in_p, tab_a, tab_b, qg, wuq, kvg, wukv, n_b, s_len)

        ya = _swa_call(swa_sink[l], qa, ka, va, ctx_len)
        yb = _mla_call(qb, kb, vb, ctx_len)

        bmat, cmat, a_r, a_i = _s5_matrices(s5_lambda_re[l], s5_lambda_im[l], s5_log_step[l],
                                            s5_b_re[l], s5_b_im[l], s5_c_re[l], s5_c_im[l])
        cu_t = jnp.transpose(cu, (1, 0, 2))
        y_dirs = _s5_call(cu_t, bmat, cmat, a_r, a_i, ctx_len)
        ys_t = _s5_glu_call(cu_t.reshape(s_len * n_b, GROUP_WIDTH), y_dirs.reshape(2, s_len * n_b, GROUP_WIDTH),
                            s5_d[l].reshape(1, GROUP_WIDTH), s5_w_glu[l].astype(BF16))
        ys = jnp.transpose(ys_t.reshape(s_len, n_b, GROUP_WIDTH), (1, 0, 2))

        yd = _na_call(qd, kd, vd, _na_bias(na_rpb[l]), ctx_len)

        mixg_p, w_out_p = _mix_layout(mix_norm_g[l], w_out[l])
        x1_all, h2_all = _out_call(ya, yb, ys, yd, x_all, modsel[l], mixg_p, w_out_p, norm2_g[l].reshape(1, d))

        n_tok = n_lat if last else x_all.shape[0]
        x_all = _peer_call(h2_all, x1_all, modsel[l], peer_w_q[l].astype(BF16), peer_sub_keys[l].astype(BF16),
                           peer_u[l].astype(BF16), peer_v[l].astype(BF16).T, fg, n_tok, n_lat, l_len, last)

    return x_all[:n_lat].reshape(n_b, l_len, d)
```

```python
import functools
import math

import numpy as np
import jax
import jax.numpy as jnp
from jax import lax
from jax.experimental import pallas as pl
from jax.experimental.pallas import tpu as pltpu

F32 = jnp.float32
BF16 = jnp.bfloat16

D_MODEL = 1024
HEAD_DIM = 64
GROUP_WIDTH = 256
GRID_W = 64
EPS = 1e-6
NEG_INF = -1e30
ROPE_BASE = 10000.0
N_HEADS = 4
A_KV_HEADS = 2
A_WINDOW = 128
A_BLOCK = 128
B_NOPE = 64
B_ROPE = 32
B_Q_LORA = 192
B_KV_LORA = 128
C_GROUP = 16
C_NGROUPS = 16
C_STATE = 64
NA_KH = 8
NA_KW = 16
PEER_HEADS = 8
PEER_TOPK = 16

LANES = 128
HEAD_SLOT = LANES
TOKEN_TILE = 256
VMEM_LIMIT = 56 * 1024 * 1024

N_STATE = C_NGROUPS * C_STATE


def _cparams(sem):
    return pltpu.CompilerParams(dimension_semantics=sem, vmem_limit_bytes=VMEM_LIMIT)


def _nt_dot(a, b):
    return lax.dot_general(a, b, (((1,), (1,)), ((), ())), preferred_element_type=F32)


def _dot(a, b):
    return jnp.dot(a, b, preferred_element_type=F32)


def _ada_kernel(c_ref, w_ref, b_ref, o_ref):
    c = c_ref[...]
    s = c * jax.nn.sigmoid(c)
    o_ref[0] = _dot(s.astype(BF16), w_ref[0].astype(BF16)) + b_ref[0]


def _ada_call(cc, w_ada, b_ada):
    depth, d, n6 = w_ada.shape
    tn = 1536
    return pl.pallas_call(
        _ada_kernel,
        grid=(depth, n6 // tn),
        in_specs=[pl.BlockSpec((cc.shape[0], d), lambda l, j: (0, 0)),
                  pl.BlockSpec((1, d, tn), lambda l, j: (l, 0, j)),
                  pl.BlockSpec((1, 1, tn), lambda l, j: (l, 0, j))],
        out_specs=pl.BlockSpec((1, cc.shape[0], tn), lambda l, j: (l, 0, j)),
        out_shape=jax.ShapeDtypeStruct((depth, cc.shape[0], n6), F32),
        compiler_params=_cparams(("arbitrary", "arbitrary")),
        name="ada_mod",
    )(cc, w_ada, b_ada.reshape(depth, 1, n6))


_C_AQ = 0
_C_AK = _C_AQ + 4 * HEAD_SLOT
_C_AV = _C_AK + 2 * HEAD_SLOT
_C_BQ = _C_AV + 2 * HEAD_SLOT
_C_BKV = _C_BQ + 256
_C_BKR = _C_BKV + 128
_C_CU = _C_BKR + 128
_C_DQ = _C_CU + 256
_C_DK = _C_DQ + 4 * HEAD_SLOT
_C_DV = _C_DK + 4 * HEAD_SLOT
_C_END = _C_DV + 4 * HEAD_SLOT


def _rms(x, n):
    return x * lax.rsqrt(jnp.sum(x * x, axis=-1, keepdims=True) * (1.0 / n) + EPS)


def _rope(t, tab_ref, shift):
    return (t * tab_ref[0]
            + pltpu.roll(t, shift, 1) * tab_ref[1]
            + pltpu.roll(t, LANES - shift, 1) * tab_ref[2])


def _proj_kernel(x_ref, mod_ref, g_ref, w_ref, ta_ref, tb_ref, qg_ref, wuq_ref, kvg_ref, wukv_ref,
                 qa_ref, ka_ref, va_ref, qb_ref, kb_ref, vb_ref, cu_ref, qd_ref, kd_ref, vd_ref):
    x = x_ref[...]
    m = mod_ref[0, 0]
    h = _rms(x, D_MODEL) * g_ref[...]
    h = h * (1.0 + m[1:2]) + m[0:1]
    z = _dot(h.astype(BF16), w_ref[...])

    a_scale = HEAD_DIM ** -0.5
    for hh in range(N_HEADS):
        c0 = _C_AQ + hh * HEAD_SLOT
        qa_ref[0, :, hh * HEAD_SLOT:(hh + 1) * HEAD_SLOT] = (
            _rope(z[:, c0:c0 + HEAD_SLOT], ta_ref, 16) * a_scale).astype(BF16)
    for g in range(A_KV_HEADS):
        c0 = _C_AK + g * HEAD_SLOT
        ka_ref[0, :, g * HEAD_SLOT:(g + 1) * HEAD_SLOT] = _rope(z[:, c0:c0 + HEAD_SLOT], ta_ref, 16).astype(BF16)
    va_ref[0] = z[:, _C_AV:_C_AV + 2 * HEAD_SLOT].astype(BF16)

    cq = _rms(z[:, _C_BQ:_C_BQ + 256], B_Q_LORA) * qg_ref[...]
    qb = _dot(cq.astype(BF16), wuq_ref[...])
    ckv = _rms(z[:, _C_BKV:_C_BKV + 128], B_KV_LORA) * kvg_ref[...]
    kv = _dot(ckv.astype(BF16), wukv_ref[...])
    kr = _rope(z[:, _C_BKR:_C_BKR + 128], tb_ref, 8)
    lane = lax.broadcasted_iota(jnp.int32, (1, HEAD_SLOT), 1)
    ones_col = (lane == HEAD_DIM).astype(F32)
    for hh in range(N_HEADS):
        sl = slice(hh * HEAD_SLOT, (hh + 1) * HEAD_SLOT)
        qb_ref[0, :, sl] = _rope(qb[:, sl], tb_ref, 8).astype(BF16)
        kb_ref[0, :, sl] = (kv[:, sl] + kr).astype(BF16)
        vsl = slice(4 * HEAD_SLOT + hh * HEAD_SLOT, 4 * HEAD_SLOT + (hh + 1) * HEAD_SLOT)
        vb_ref[0, :, sl] = (kv[:, vsl] + ones_col).astype(BF16)

    cu_ref[0] = z[:, _C_CU:_C_CU + 256]
    qd_ref[0] = (z[:, _C_DQ:_C_DQ + 4 * HEAD_SLOT] * a_scale).astype(BF16)
    kd_ref[0] = z[:, _C_DK:_C_DK + 4 * HEAD_SLOT].astype(BF16)
    vd_ref[0] = z[:, _C_DV:_C_DV + 4 * HEAD_SLOT].astype(BF16)


def _stream_block(b, t, n_b, lat_tiles):
    return jnp.where(t == 0, n_b * lat_tiles + b, b * lat_tiles + t - 1)


def _proj_call(x_all, modsel, g1, w_in_p, tab_a, tab_b, qg, wuq, kvg, wukv, n_b, s_len):
    tiles = s_len // TOKEN_TILE
    lat_tiles = tiles - 1
    T = TOKEN_TILE
    xmap = lambda b, t: (_stream_block(b, t, n_b, lat_tiles), 0)
    full = lambda shape: pl.BlockSpec(shape, lambda b, t: (0,) * len(shape))
    omap = lambda b, t: (b, t, 0)
    widths = [4 * HEAD_SLOT, 2 * HEAD_SLOT, 2 * HEAD_SLOT, 4 * HEAD_SLOT, 4 * HEAD_SLOT, 4 * HEAD_SLOT,
              256, 4 * HEAD_SLOT, 4 * HEAD_SLOT, 4 * HEAD_SLOT]
    dts = [BF16, BF16, BF16, BF16, BF16, BF16, F32, BF16, BF16, BF16]
    return pl.pallas_call(
        _proj_kernel,
        grid=(n_b, tiles),
        in_specs=[pl.BlockSpec((T, D_MODEL), xmap),
                  pl.BlockSpec((1, 1, 6, D_MODEL), lambda b, t: (b, jnp.minimum(t, 1), 0, 0)),
                  full((1, D_MODEL)),
                  full(w_in_p.shape),
                  pl.BlockSpec((3, T, LANES), lambda b, t: (0, t, 0)),
                  pl.BlockSpec((3, T, LANES), lambda b, t: (0, t, 0)),
                  full(qg.shape), full(wuq.shape), full(kvg.shape), full(wukv.shape)],
        out_specs=[pl.BlockSpec((1, T, w), omap) for w in widths],
        out_shape=[jax.ShapeDtypeStruct((n_b, s_len, w), dt) for w, dt in zip(widths, dts)],
        compiler_params=_cparams(("arbitrary", "arbitrary")),
        name="mod_proj",
    )(x_all, modsel, g1, w_in_p, tab_a, tab_b, qg, wuq, kvg, wukv)


def _swa_kernel(sink_ref, q_ref, k_ref, v_ref, o_ref, *, ctx_len, s_len):
    n = pl.program_id(1)
    is_lat = n >= ctx_len // A_BLOCK
    band = 3 * A_BLOCK
    ks = pl.multiple_of(jnp.clip((n - 1) * A_BLOCK, 0, s_len - band), A_BLOCK)
    q = q_ref[0]
    kl = k_ref[0, pl.ds(ks, band), :]
    vl = v_ref[0, pl.ds(ks, band), :]
    kc = k_ref[0, 0:ctx_len, :]
    vc = v_ref[0, 0:ctx_len, :]
    qpos = n * A_BLOCK + lax.broadcasted_iota(jnp.int32, (A_BLOCK, band), 0)
    kpos = ks + lax.broadcasted_iota(jnp.int32, (A_BLOCK, band), 1)
    window = jnp.where(is_lat, A_WINDOW, -1)
    valid = (jnp.abs(qpos - kpos) <= window) & (kpos >= ctx_len)
    rep = N_HEADS // A_KV_HEADS
    for hh in range(N_HEADS):
        g = hh // rep
        qs = slice(hh * HEAD_SLOT, (hh + 1) * HEAD_SLOT)
        gs = slice(g * HEAD_SLOT, (g + 1) * HEAD_SLOT)
        s_loc = jnp.where(valid, _nt_dot(q[:, qs], kl[:, gs]), NEG_INF)
        s_ctx = _nt_dot(q[:, qs], kc[:, gs])
        sk = sink_ref[hh]
        mx = jnp.maximum(jnp.max(s_loc, axis=-1, keepdims=True), jnp.max(s_ctx, axis=-1, keepdims=True))
        mx = jnp.maximum(mx, sk)
        p_loc = jnp.exp(s_loc - mx)
        p_ctx = jnp.exp(s_ctx - mx)
        den = (jnp.sum(p_loc, axis=-1, keepdims=True) + jnp.sum(p_ctx, axis=-1, keepdims=True)
               + jnp.exp(sk - mx))
        o = _dot(p_loc.astype(BF16), vl[:, gs]) + _dot(p_ctx.astype(BF16), vc[:, gs])
        o_ref[0, :, qs] = o * (1.0 / den)


def _swa_call(sink, qa, ka, va, ctx_len):
    n_b, s_len, _ = qa.shape
    kern = functools.partial(_swa_kernel, ctx_len=ctx_len, s_len=s_len)
    return pl.pallas_call(
        kern,
        grid=(n_b, s_len // A_BLOCK),
        in_specs=[pl.BlockSpec(memory_space=pltpu.SMEM),
                  pl.BlockSpec((1, A_BLOCK, 4 * HEAD_SLOT), lambda b, n: (b, n, 0)),
                  pl.BlockSpec((1, s_len, 2 * HEAD_SLOT), lambda b, n: (b, 0, 0)),
                  pl.BlockSpec((1, s_len, 2 * HEAD_SLOT), lambda b, n: (b, 0, 0))],
        out_specs=pl.BlockSpec((1, A_BLOCK, 4 * HEAD_SLOT), lambda b, n: (b, n, 0)),
        out_shape=jax.ShapeDtypeStruct((n_b, s_len, 4 * HEAD_SLOT), F32),
        compiler_params=_cparams(("arbitrary", "arbitrary")),
        name="swa_mixer",
    )(sink, qa, ka, va)


MLA_TQ = 256
MLA_CK = 512


def _mla_kernel(q_ref, k_ref, v_ref, o_ref, m_scr, acc_scr, *, ctx_len, s_len):
    t = pl.program_id(1)
    is_lat = t >= ctx_len // MLA_TQ
    scale = (B_NOPE + B_ROPE) ** -0.5
    n_chunks = (s_len - ctx_len) // MLA_CK
    lane = lax.broadcasted_iota(jnp.int32, (1, HEAD_SLOT), 1)

    for hh in range(N_HEADS):
        sl = slice(hh * HEAD_SLOT, (hh + 1) * HEAD_SLOT)
        q = q_ref[0, :, sl]

        def chunk(start, size, first):
            k = k_ref[0, pl.ds(start, size), sl]
            v = v_ref[0, pl.ds(start, size), sl]
            s = _nt_dot(q, k) * scale
            m_cur = jnp.max(s, axis=-1, keepdims=True)
            if first:
                m_new = jnp.broadcast_to(m_cur, (MLA_TQ, LANES))
                p = jnp.exp(s - jnp.concatenate([m_new] * (size // LANES), axis=1))
                acc_scr[...] = _dot(p.astype(BF16), v)
            else:
                m_prev = m_scr[...]
                m_new = jnp.maximum(m_prev, m_cur)
                alpha = jnp.exp(m_prev - m_new)
                p = jnp.exp(s - jnp.concatenate([m_new] * (size // LANES), axis=1))
                acc_scr[...] = acc_scr[...] * alpha + _dot(p.astype(BF16), v)
            m_scr[...] = m_new

        chunk(0, ctx_len, True)

        @pl.when(is_lat)
        def _():
            def body(c, carry):
                chunk(pl.multiple_of(ctx_len + c * MLA_CK, MLA_CK // 2), MLA_CK, False)
                return carry
            lax.fori_loop(0, n_chunks, body, 0)

        acc = acc_scr[...]
        den = acc[:, HEAD_DIM:HEAD_DIM + 1]
        o_ref[0, :, sl] = jnp.where(lane < HEAD_DIM, acc * (1.0 / den), 0.0)


def _mla_call(qb, kb, vb, ctx_len):
    n_b, s_len, _ = qb.shape
    kern = functools.partial(_mla_kernel, ctx_len=ctx_len, s_len=s_len)
    return pl.pallas_call(
        kern,
        grid=(n_b, s_len // MLA_TQ),
        in_specs=[pl.BlockSpec((1, MLA_TQ, 4 * HEAD_SLOT), lambda b, t: (b, t, 0)),
                  pl.BlockSpec((1, s_len, 4 * HEAD_SLOT), lambda b, t: (b, 0, 0)),
                  pl.BlockSpec((1, s_len, 4 * HEAD_SLOT), lambda b, t: (b, 0, 0))],
        out_specs=pl.BlockSpec((1, MLA_TQ, 4 * HEAD_SLOT), lambda b, t: (b, t, 0)),
        out_shape=jax.ShapeDtypeStruct((n_b, s_len, 4 * HEAD_SLOT), F32),
        scratch_shapes=[pltpu.VMEM((MLA_TQ, LANES), F32), pltpu.VMEM((MLA_TQ, HEAD_SLOT), F32)],
        compiler_params=_cparams(("arbitrary", "arbitrary")),
        name="mla_mixer",
    )(qb, kb, vb)


S5_T = 128


def _s5_kernel(u_ref, bm_ref, cm_ref, ar_ref, ai_ref, y_ref, hr_scr, hi_scr, bu_scr, *, n_b):
    d = pl.program_id(0)
    c = pl.program_id(1)

    @pl.when(c == 0)
    def _():
        hr_scr[...] = jnp.zeros_like(hr_scr)
        hi_scr[...] = jnp.zeros_like(hi_scr)

    u = u_ref[...].reshape(S5_T * n_b, GROUP_WIDTH)
    bu_scr[...] = _dot(u.astype(BF16), bm_ref[0])
    a_r = jnp.broadcast_to(ar_ref[0], (n_b, N_STATE))
    a_i = jnp.broadcast_to(ai_ref[0], (n_b, N_STATE))

    def step(i, carry):
        h_r, h_i = carry
        t = jnp.where(d == 0, i, S5_T - 1 - i)
        row = pl.multiple_of(t * n_b, n_b)
        b_r = bu_scr[pl.ds(row, n_b), 0:N_STATE]
        b_i = bu_scr[pl.ds(row, n_b), N_STATE:2 * N_STATE]
        n_r = a_r * h_r - a_i * h_i + b_r
        n_i = a_r * h_i + a_i * h_r + b_i
        bu_scr[pl.ds(row, n_b), 0:N_STATE] = n_r
        bu_scr[pl.ds(row, n_b), N_STATE:2 * N_STATE] = n_i
        return n_r, n_i

    h_r, h_i = lax.fori_loop(0, S5_T, step, (hr_scr[...], hi_scr[...]), unroll=4)
    hr_scr[...] = h_r
    hi_scr[...] = h_i
    y = _dot(bu_scr[...].astype(BF16), cm_ref[0])
    y_ref[0] = y.reshape(S5_T, n_b, GROUP_WIDTH)


def _s5_chunk(d, c, n_chunks, ctx_chunks):
    rev = jnp.where(c < ctx_chunks, ctx_chunks - 1 - c, n_chunks - 1 - (c - ctx_chunks))
    return jnp.where(d == 0, c, rev)


def _s5_call(cu_t, bmat, cmat, a_r, a_i, ctx_len):
    s_len, n_b, _ = cu_t.shape
    n_chunks = s_len // S5_T
    ctx_chunks = ctx_len // S5_T
    cmap = lambda d, c: (_s5_chunk(d, c, n_chunks, ctx_chunks), 0, 0)
    kern = functools.partial(_s5_kernel, n_b=n_b)
    return pl.pallas_call(
        kern,
        grid=(2, n_chunks),
        in_specs=[pl.BlockSpec((S5_T, n_b, GROUP_WIDTH), cmap),
                  pl.BlockSpec((1, GROUP_WIDTH, 2 * N_STATE), lambda d, c: (d, 0, 0)),
                  pl.BlockSpec((1, 2 * N_STATE, GROUP_WIDTH), lambda d, c: (d, 0, 0)),
                  pl.BlockSpec((1, 1, N_STATE), lambda d, c: (d, 0, 0)),
                  pl.BlockSpec((1, 1, N_STATE), lambda d, c: (d, 0, 0))],
        out_specs=pl.BlockSpec((1, S5_T, n_b, GROUP_WIDTH),
                               lambda d, c: (d, _s5_chunk(d, c, n_chunks, ctx_chunks), 0, 0)),
        out_shape=jax.ShapeDtypeStruct((2, s_len, n_b, GROUP_WIDTH), F32),
        scratch_shapes=[pltpu.VMEM((n_b, N_STATE), F32), pltpu.VMEM((n_b, N_STATE), F32),
                        pltpu.VMEM((S5_T * n_b, 2 * N_STATE), F32)],
        compiler_params=_cparams(("arbitrary", "arbitrary")),
        name="s5_scan",
    )(cu_t, bmat, cmat, a_r, a_i)


def _s5_glu_kernel(u_ref, y_ref, d_ref, w_ref, o_ref):
    y = u_ref[...] * d_ref[...] + y_ref[0] + y_ref[1]
    z = _dot(jax.nn.gelu(y).astype(BF16), w_ref[...])
    o_ref[...] = z[:, :GROUP_WIDTH] * jax.nn.sigmoid(z[:, GROUP_WIDTH:])


def _s5_glu_call(cu_flat, y_dirs, d_skip, w_glu):
    n = cu_flat.shape[0]
    tr = 1024
    return pl.pallas_call(
        _s5_glu_kernel,
        grid=(n // tr,),
        in_specs=[pl.BlockSpec((tr, GROUP_WIDTH), lambda i: (i, 0)),
                  pl.BlockSpec((2, tr, GROUP_WIDTH), lambda i: (0, i, 0)),
                  pl.BlockSpec((1, GROUP_WIDTH), lambda i: (0, 0)),
                  pl.BlockSpec((GROUP_WIDTH, 2 * GROUP_WIDTH), lambda i: (0, 0))],
        out_specs=pl.BlockSpec((tr, GROUP_WIDTH), lambda i: (i, 0)),
        out_shape=jax.ShapeDtypeStruct((n, GROUP_WIDTH), F32),
        compiler_params=_cparams(("arbitrary",)),
        name="s5_glu",
    )(cu_flat, y_dirs, d_skip, w_glu)


NA_KEYS = NA_KH * GRID_W


def _na_kernel(q_ref, k_ref, v_ref, bias_ref, o_ref, *, ctx_len, n_rows):
    j = pl.program_id(1)
    ctx_steps = ctx_len // GRID_W
    r0 = jnp.clip(j - ctx_steps - NA_KH // 2, 0, n_rows - NA_KH)
    ks = pl.multiple_of(ctx_len + r0 * GRID_W, GRID_W)
    q = q_ref[0]
    kl = k_ref[0, pl.ds(ks, NA_KEYS), :]
    vl = v_ref[0, pl.ds(ks, NA_KEYS), :]
    kc = k_ref[0, 0:ctx_len, :]
    vc = v_ref[0, 0:ctx_len, :]
    for hh in range(N_HEADS):
        sl = slice(hh * HEAD_SLOT, (hh + 1) * HEAD_SLOT)
        s_loc = _nt_dot(q[:, sl], kl[:, sl]) + bias_ref[0, hh]
        s_ctx = _nt_dot(q[:, sl], kc[:, sl])
        mx = jnp.maximum(jnp.max(s_loc, axis=-1, keepdims=True), jnp.max(s_ctx, axis=-1, keepdims=True))
        p_loc = jnp.exp(s_loc - mx)
        p_ctx = jnp.exp(s_ctx - mx)
        den = jnp.sum(p_loc, axis=-1, keepdims=True) + jnp.sum(p_ctx, axis=-1, keepdims=True)
        o = _dot(p_loc.astype(BF16), vl[:, sl]) + _dot(p_ctx.astype(BF16), vc[:, sl])
        o_ref[0, :, sl] = o * (1.0 / den)


def _na_call(qd, kd, vd, bias, ctx_len):
    n_b, s_len, _ = qd.shape
    n_rows = (s_len - ctx_len) // GRID_W
    ctx_steps = ctx_len // GRID_W
    kern = functools.partial(_na_kernel, ctx_len=ctx_len, n_rows=n_rows)

    def bias_map(b, j):
        r = j - ctx_steps
        var = r - jnp.clip(r - NA_KH // 2, 0, n_rows - NA_KH)
        return (jnp.where(j < ctx_steps, NA_KH, var), 0, 0, 0)

    return pl.pallas_call(
        kern,
        grid=(n_b, ctx_steps + n_rows),
        in_specs=[pl.BlockSpec((1, GRID_W, 4 * HEAD_SLOT), lambda b, j: (b, j, 0)),
                  pl.BlockSpec((1, s_len, 4 * HEAD_SLOT), lambda b, j: (b, 0, 0)),
                  pl.BlockSpec((1, s_len, 4 * HEAD_SLOT), lambda b, j: (b, 0, 0)),
                  pl.BlockSpec((1, N_HEADS, GRID_W, NA_KEYS), bias_map)],
        out_specs=pl.BlockSpec((1, GRID_W, 4 * HEAD_SLOT), lambda b, j: (b, j, 0)),
        out_shape=jax.ShapeDtypeStruct((n_b, s_len, 4 * HEAD_SLOT), F32),
        compiler_params=_cparams(("arbitrary", "arbitrary")),
        name="na_mixer",
    )(qd, kd, vd, bias)


def _out_kernel(ya_ref, yb_ref, ys_ref, yd_ref, x_ref, mod_ref, mg_ref, w_ref, g2_ref, x1_ref, h2_ref):
    m = mod_ref[0, 0]
    mg = mg_ref[...]
    parts = []
    off = 0
    for ref, width in ((ya_ref, 4 * HEAD_SLOT), (yb_ref, 4 * HEAD_SLOT), (ys_ref, GROUP_WIDTH),
                       (yd_ref, 4 * HEAD_SLOT)):
        parts.append((_rms(ref[0], GROUP_WIDTH) * mg[:, off:off + width]).astype(BF16))
        off += width
    y = _dot(jnp.concatenate(parts, axis=-1), w_ref[...])
    x1 = x_ref[...] + m[2:3] * y
    x1_ref[...] = x1
    h2 = _rms(x1, D_MODEL) * g2_ref[...]
    h2_ref[...] = (h2 * (1.0 + m[4:5]) + m[3:4]).astype(BF16)


def _out_call(ya, yb, ys, yd, x_all, modsel, mixg_p, w_out_p, g2):
    n_b, s_len, _ = ya.shape
    tiles = s_len // TOKEN_TILE
    lat_tiles = tiles - 1
    T = TOKEN_TILE
    xmap = lambda b, t: (_stream_block(b, t, n_b, lat_tiles), 0)
    full = lambda shape: pl.BlockSpec(shape, lambda b, t: (0,) * len(shape))
    ymap = lambda b, t: (b, t, 0)
    return pl.pallas_call(
        _out_kernel,
        grid=(n_b, tiles),
        in_specs=[pl.BlockSpec((1, T, ya.shape[-1]), ymap), pl.BlockSpec((1, T, yb.shape[-1]), ymap),
                  pl.BlockSpec((1, T, ys.shape[-1]), ymap), pl.BlockSpec((1, T, yd.shape[-1]), ymap),
                  pl.BlockSpec((T, D_MODEL), xmap),
                  pl.BlockSpec((1, 1, 6, D_MODEL), lambda b, t: (b, jnp.minimum(t, 1), 0, 0)),
                  full(mixg_p.shape), full(w_out_p.shape), full((1, D_MODEL))],
        out_specs=[pl.BlockSpec((T, D_MODEL), xmap), pl.BlockSpec((T, D_MODEL), xmap)],
        out_shape=[jax.ShapeDtypeStruct(x_all.shape, F32), jax.ShapeDtypeStruct(x_all.shape, BF16)],
        compiler_params=_cparams(("arbitrary", "arbitrary")),
        name="out_proj",
    )(ya, yb, ys, yd, x_all, modsel, mixg_p, w_out_p, g2)


PEER_TT = 512
PEER_EB = 1024
PEER_SB = 256
PEER_CW = 256
PEER_PAD = 16


def _peer_kernel(h2_ref, x1_ref, mod_ref, wq_ref, sk_ref, u_ref, vt_ref, fg_ref, o_ref,
                 h2t_scr, q_scr, g_scr, n_scr, c1_scr, v1_scr, v2_scr, z_scr, p_scr, out_scr,
                 *, n_keys, final):
    e = pl.program_id(1)
    n_e = pl.num_programs(1)
    TT = PEER_TT
    LT = TT // LANES
    NK = n_keys
    e2_row0 = NK + PEER_PAD
    neg = -jnp.inf

    @pl.when(e == 0)
    def _():
        h2 = h2_ref[...]
        h2t_scr[...] = h2.astype(F32).T.astype(BF16)
        q_scr[...] = _dot(h2, wq_ref[...]).astype(BF16)
        out_scr[...] = jnp.zeros_like(out_scr)

        def extract(s, v_scr, ls, want_rank):
            w = s
            rank = jnp.full(s.shape, 127.0, F32) if want_rank else None
            for k in range(PEER_TOPK):
                mk = jnp.max(w, axis=0, keepdims=True)
                v_scr[k:k + 1, ls] = mk
                hit = w == mk
                if want_rank:
                    rank = jnp.where(hit, float(k), rank)
                if k + 1 < PEER_TOPK:
                    w = jnp.where(hit, neg, w)
            return rank

        def head(hh, carry):
            c0 = pl.multiple_of(hh * 256, 256)
            s1_all = _nt_dot(sk_ref[0], q_scr[:, pl.ds(c0, 128)])
            s2_all = _nt_dot(sk_ref[1], q_scr[:, pl.ds(c0 + 128, 128)])
            for lt in range(LT):
                ls = slice(lt * LANES, (lt + 1) * LANES)
                s1 = s1_all[:, ls]
                s2 = s2_all[:, ls]
                extract(s1, v1_scr, ls, False)
                r2 = extract(s2, v2_scr, ls, True)
                slabs = [v1_scr[0:1, ls] + v2_scr[0:16, ls]]
                for a in range(1, 5):
                    slabs.append(v1_scr[a:a + 1, ls] + v2_scr[0:8, ls])
                slabs.append(v1_scr[0:8, ls] + v2_scr[1:2, ls])
                slabs.append(v1_scr[0:8, ls] + v2_scr[0:1, ls])
                slabs.append(v1_scr[8:16, ls] + v2_scr[0:1, ls])
                cand = jnp.concatenate(slabs, axis=0)
                top = v1_scr[0:1, ls] + v2_scr[0:1, ls]
                zsum = jnp.zeros((1, LANES), F32)
                tau = top
                for k in range(PEER_TOPK):
                    tau = jnp.max(cand, axis=0, keepdims=True)
                    zsum = zsum + jnp.exp(tau - top)
                    if k + 1 < PEER_TOPK:
                        cand = jnp.where(cand == tau, neg, cand)
                count = jnp.zeros(s1.shape, F32)
                for a in range(PEER_TOPK):
                    v1a = v1_scr[a:a + 1, ls]
                    sel = (v1a + v2_scr[0:16, ls]) >= tau
                    cnt = jnp.sum(jnp.where(sel, 1.0, 0.0), axis=0, keepdims=True)
                    count = jnp.where(s1 == v1a, cnt, count)
                n_scr[hh, :, ls] = count
                c1_scr[hh, :, ls] = jnp.exp(s1 - v1_scr[0:1, ls]) * (1.0 / zsum)
                g_scr[lt, hh, 0:NK, :] = r2.astype(BF16)
                g_scr[lt, hh, e2_row0:e2_row0 + NK, :] = jnp.exp(s2 - v2_scr[0:1, ls]).astype(BF16)
            return carry

        lax.fori_loop(0, PEER_HEADS, head, 0)

    EB = u_ref.shape[0]
    i_per = EB // NK
    i_per_sb = PEER_SB // NK
    jt_n = NK // 16
    lt_per = PEER_CW // LANES
    chunks = [(sb, hf) for sb in range(EB // PEER_SB) for hf in range(TT // PEER_CW)]

    def stage_z(c):
        sb, hf = chunks[c]
        rows = slice(sb * PEER_SB, (sb + 1) * PEER_SB)
        z = _dot(u_ref[rows, :], h2t_scr[:, hf * PEER_CW:(hf + 1) * PEER_CW])
        for l in range(lt_per):
            z_scr[hf * lt_per + l, rows, :] = z[:, l * LANES:(l + 1) * LANES]

    def stage_gate(c):
        sb, hf = chunks[c]
        for lt in range(hf * lt_per, (hf + 1) * lt_per):
            ls = slice(lt * LANES, (lt + 1) * LANES)
            for i2 in range(i_per_sb):
                ii = sb * i_per_sb + i2
                i8 = pl.multiple_of(e * i_per + (ii // 8) * 8, 8)
                r8 = slice(ii % 8, ii % 8 + 1)
                nrow = [jnp.broadcast_to(n_scr[hh, pl.ds(i8, 8), ls][r8], (16, LANES)).astype(BF16)
                        for hh in range(PEER_HEADS)]
                crow = [jnp.broadcast_to(c1_scr[hh, pl.ds(i8, 8), ls][r8], (16, LANES)).astype(BF16)
                        for hh in range(PEER_HEADS)]
                for jt in range(jt_n):
                    w = None
                    for hh in range(PEER_HEADS):
                        r2t = g_scr[lt, hh, jt * 16:(jt + 1) * 16, :]
                        e2t = g_scr[lt, hh, e2_row0 + jt * 16:e2_row0 + (jt + 1) * 16, :]
                        term = jnp.where(r2t < nrow[hh], e2t, jnp.zeros_like(e2t)) * crow[hh]
                        w = term if w is None else w + term
                    zr = slice(sb * PEER_SB + i2 * NK + jt * 16, sb * PEER_SB + i2 * NK + (jt + 1) * 16)
                    act = jax.nn.gelu(z_scr[lt, zr, :])
                    p_scr[lt, zr, :] = w * act.astype(BF16)

    def stage_out(c):
        sb, hf = chunks[c]
        rows = slice(sb * PEER_SB, (sb + 1) * PEER_SB)
        p_c = jnp.concatenate([p_scr[hf * lt_per + l, rows, :] for l in range(lt_per)], axis=1)
        cs = slice(hf * PEER_CW, (hf + 1) * PEER_CW)
        out_scr[:, cs] += _dot(vt_ref[:, rows], p_c)

    stage_z(0)
    stage_z(1)
    for c in range(len(chunks)):
        stage_gate(c)
        stage_out(c)
        if c + 2 < len(chunks):
            stage_z(c + 2)

    @pl.when(e == n_e - 1)
    def _():
        x2 = x1_ref[...] + mod_ref[0, 0][5:6] * out_scr[...].T
        if final:
            x2 = _rms(x2, D_MODEL) * fg_ref[...]
        o_ref[...] = x2


def _peer_call(h2_all, x1_all, modsel, wq, subk, u_tab, vt_tab, fg, n_tok, n_lat, l_len, final):
    n_exp = u_tab.shape[0]
    n_keys = subk.shape[1]
    assert n_exp == n_keys * n_keys and n_keys % 16 == 0
    assert min(PEER_EB, n_exp) % (8 * n_keys) == 0
    assert PEER_SB % n_keys == 0 and min(PEER_EB, n_exp) % PEER_SB == 0
    assert n_tok % PEER_TT == 0 and n_lat % PEER_TT == 0 and l_len % PEER_TT == 0
    TT, EB = PEER_TT, min(PEER_EB, n_exp)
    lat_tiles = n_lat // TT
    per_b = l_len // TT

    def mod_map(i, e):
        return (jnp.where(i < lat_tiles, i // per_b, 0), (i < lat_tiles).astype(jnp.int32), 0, 0)

    kern = functools.partial(_peer_kernel, n_keys=n_keys, final=final)
    full = lambda shape: pl.BlockSpec(shape, lambda i, e: (0,) * len(shape))
    return pl.pallas_call(
        kern,
        grid=(n_tok // TT, n_exp // EB),
        in_specs=[pl.BlockSpec((TT, D_MODEL), lambda i, e: (i, 0)),
                  pl.BlockSpec((TT, D_MODEL), lambda i, e: (i, 0)),
                  pl.BlockSpec((1, 1, 6, D_MODEL), mod_map),
                  full(wq.shape), full(subk.shape),
                  pl.BlockSpec((EB, D_MODEL), lambda i, e: (e, 0)),
                  pl.BlockSpec((D_MODEL, EB), lambda i, e: (0, e)),
                  full((1, D_MODEL))],
        out_specs=pl.BlockSpec((TT, D_MODEL), lambda i, e: (i, 0)),
        out_shape=jax.ShapeDtypeStruct((n_tok, D_MODEL), F32),
        scratch_shapes=[pltpu.VMEM((D_MODEL, TT), BF16),
                        pltpu.VMEM((TT, PEER_HEADS * 256), BF16),
                        pltpu.VMEM((TT // LANES, PEER_HEADS, 2 * (n_keys + PEER_PAD), LANES), BF16),
                        pltpu.VMEM((PEER_HEADS, n_keys, TT), F32),
                        pltpu.VMEM((PEER_HEADS, n_keys, TT), F32),
                        pltpu.VMEM((PEER_TOPK, TT), F32),
                        pltpu.VMEM((PEER_TOPK, TT), F32),
                        pltpu.VMEM((TT // LANES, EB, LANES), F32),
                        pltpu.VMEM((TT // LANES, EB, LANES), BF16),
                        pltpu.VMEM((D_MODEL, TT), F32)],
        compiler_params=_cparams(("arbitrary", "arbitrary")),
        name="peer_ffn",
    )(h2_all, x1_all, modsel, wq, subk, u_tab, vt_tab, fg)


def _head_slots(w, n_heads, width=HEAD_DIM):
    lead = w.shape[:-1]
    w = w.reshape(lead + (n_heads, width))
    w = jnp.pad(w, [(0, 0)] * len(lead) + [(0, 0), (0, HEAD_SLOT - width)])
    return w.reshape(lead + (n_heads * HEAD_SLOT,))


def _prep_w_in(w_in):
    sizes = (256, 128, 128, B_Q_LORA, B_KV_LORA, B_ROPE, 256, 256, 256, 256)
    offs = np.cumsum((0,) + sizes)
    aq, ak, av, bq, bkv, bkr, cu, dq, dk, dv = [w_in[:, offs[i]:offs[i + 1]] for i in range(10)]
    d = w_in.shape[0]
    bkr_p = jnp.concatenate([jnp.zeros((d, B_NOPE), F32), bkr, jnp.zeros((d, HEAD_SLOT - B_NOPE - B_ROPE), F32)], 1)
    cols = [_head_slots(aq, 4), _head_slots(ak, 2), _head_slots(av, 2),
            jnp.pad(bq, ((0, 0), (0, 256 - B_Q_LORA))), bkv, bkr_p, cu,
            _head_slots(dq, 4), _head_slots(dk, 4), _head_slots(dv, 4)]
    w = jnp.concatenate(cols, axis=1)
    assert w.shape[1] == _C_END
    return w.astype(BF16)


def _prep_mla(w_uq, w_ukv, qg, kvg):
    wq = _head_slots(w_uq, N_HEADS, B_NOPE + B_ROPE)
    wq = jnp.pad(wq, ((0, 256 - B_Q_LORA), (0, 0))).astype(BF16)
    kv = w_ukv.reshape(B_KV_LORA, N_HEADS, B_NOPE + HEAD_DIM)
    wk = _head_slots(kv[:, :, :B_NOPE].reshape(B_KV_LORA, -1), N_HEADS, B_NOPE)
    wv = _head_slots(kv[:, :, B_NOPE:].reshape(B_KV_LORA, -1), N_HEADS, HEAD_DIM)
    wkv = jnp.concatenate([wk, wv], axis=1).astype(BF16)
    qg_p = jnp.pad(qg, (0, 256 - B_Q_LORA)).reshape(1, 256)
    return wq, wkv, qg_p, kvg.reshape(1, B_KV_LORA)


def _rope_tables(ctx_len, l_len):
    t = np.arange(l_len)
    pos = np.stack([t // GRID_W, t % GRID_W], 0).astype(np.float64)

    def build(width, lane0):
        a = width // 2
        half = a // 2
        inv = ROPE_BASE ** (-np.arange(half, dtype=np.float64) / half)
        cos = np.ones((l_len, LANES)); sp = np.zeros((l_len, LANES)); sm = np.zeros((l_len, LANES))
        for j in range(width):
            axis, i = j // a, j % a
            f, second = i % half, i >= half
            ang = pos[axis] * inv[f]
            cos[:, lane0 + j] = np.cos(ang)
            if second:
                sp[:, lane0 + j] = np.sin(ang)
            else:
                sm[:, lane0 + j] = -np.sin(ang)
        tab = np.stack([cos, sp, sm], 0)
        ctx = np.stack([np.ones((ctx_len, LANES)), np.zeros((ctx_len, LANES)), np.zeros((ctx_len, LANES))], 0)
        return jnp.asarray(np.concatenate([ctx, tab], axis=1), F32)

    return build(HEAD_DIM, 0), build(B_ROPE, B_NOPE)


def _s5_matrices(lam_re, lam_im, log_step, b_re, b_im, c_re, c_im):
    lr, li = lam_re.astype(F32), lam_im.astype(F32)
    step = jnp.exp(log_step.astype(F32))[..., None]
    mag = jnp.exp(lr * step)
    abr, abi = mag * jnp.cos(li * step), mag * jnp.sin(li * step)
    nr, ni = abr - 1.0, abi
    den = lr * lr + li * li
    fr = (nr * lr + ni * li) / den
    fi = (ni * lr - nr * li) / den
    bbr = fr[..., None] * b_re - fi[..., None] * b_im
    bbi = fr[..., None] * b_im + fi[..., None] * b_re
    eye = jnp.eye(C_NGROUPS, dtype=F32)
    bm_r = jnp.einsum('dgpc,gh->dgchp', bbr, eye).reshape(2, GROUP_WIDTH, N_STATE)
    bm_i = jnp.einsum('dgpc,gh->dgchp', bbi, eye).reshape(2, GROUP_WIDTH, N_STATE)
    bmat = jnp.concatenate([bm_r, bm_i], axis=2).astype(BF16)
    cm_r = jnp.einsum('dgcp,gh->dgphc', c_re.astype(F32), eye).reshape(2, N_STATE, GROUP_WIDTH)
    cm_i = jnp.einsum('dgcp,gh->dgphc', c_im.astype(F32), eye).reshape(2, N_STATE, GROUP_WIDTH)
    cmat = jnp.concatenate([cm_r, -cm_i], axis=1).astype(BF16)
    return bmat, cmat, abr.reshape(2, 1, N_STATE), abi.reshape(2, 1, N_STATE)


def _na_bias(rpb):
    w = np.arange(GRID_W)
    cs = np.clip(w - NA_KW // 2, 0, GRID_W - NA_KW)
    c = np.arange(GRID_W)
    inwin = (c[None, :] >= cs[:, None]) & (c[None, :] < cs[:, None] + NA_KW)
    colidx = np.clip(c[None, :] - w[:, None] + NA_KW - 1, 0, 2 * NA_KW - 2)
    y = np.arange(NA_KH)
    variants = []
    for d in range(NA_KH):
        rowidx = y - d + NA_KH - 1
        b = rpb.astype(F32)[:, rowidx][:, :, colidx]
        b = jnp.where(jnp.asarray(inwin)[None, None], b, NEG_INF)
        variants.append(jnp.transpose(b, (0, 2, 1, 3)).reshape(N_HEADS, GRID_W, NA_KEYS))
    variants.append(jnp.full((N_HEADS, GRID_W, NA_KEYS), NEG_INF, F32))
    return jnp.stack(variants, 0)


def _mix_layout(mix_norm_g, w_out):
    g = mix_norm_g.reshape(4, GROUP_WIDTH)
    w = w_out.reshape(4, GROUP_WIDTH, D_MODEL)
    gs, ws = [], []
    for k in range(4):
        if k == 2:
            gs.append(g[k]); ws.append(w[k])
        else:
            gs.append(_head_slots(g[k], N_HEADS))
            wk = w[k].reshape(N_HEADS, HEAD_DIM, D_MODEL)
            wk = jnp.pad(wk, ((0, 0), (0, HEAD_SLOT - HEAD_DIM), (0, 0))).reshape(N_HEADS * HEAD_SLOT, D_MODEL)
            ws.append(wk)
    return jnp.concatenate(gs).reshape(1, -1), jnp.concatenate(ws, axis=0).astype(BF16)


def kernel(x, c, ctx, c_ctx, norm1_g, norm2_g, w_ada, b_ada, w_in, swa_sink, mla_q_norm_g, mla_w_uq,
           mla_kv_norm_g, mla_w_ukv, s5_lambda_re, s5_lambda_im, s5_log_step, s5_b_re, s5_b_im, s5_c_re,
           s5_c_im, s5_d, s5_w_glu, na_rpb, mix_norm_g, w_out, peer_w_q, peer_sub_keys, peer_u, peer_v,
           final_norm_g):
    n_b, l_len, d = x.shape
    ctx_len = ctx.shape[1]
    depth = w_in.shape[0]
    s_len = ctx_len + l_len
    assert d == D_MODEL and ctx_len == TOKEN_TILE and n_b % 8 == 0 and n_b < 16
    assert l_len % MLA_CK == 0 and l_len // GRID_W >= NA_KH and l_len % PEER_TT == 0
    n_lat = n_b * l_len

    cc = jnp.concatenate([c, c_ctx[None], jnp.zeros((16 - n_b - 1, d), F32)], axis=0)
    mod = _ada_call(cc, w_ada, b_ada).reshape(depth, 16, 6, d)
    lat_mod = mod[:, :n_b]
    ctx_mod = jnp.broadcast_to(mod[:, n_b:n_b + 1], lat_mod.shape)
    modsel = jnp.stack([ctx_mod, lat_mod], axis=2)

    tab_a, tab_b = _rope_tables(ctx_len, l_len)
    x_all = jnp.concatenate([x.reshape(n_lat, d), ctx.reshape(n_b * ctx_len, d)], axis=0)
    fg = final_norm_g.reshape(1, d)

    for l in range(depth):
        last = l == depth - 1
        w_in_p = _prep_w_in(w_in[l])
        wuq, wukv, qg, kvg = _prep_mla(mla_w_uq[l], mla_w_ukv[l], mla_q_norm_g[l], mla_kv_norm_g[l])
        qa, ka, va, qb, kb, vb, cu, qd, kd, vd = _proj_call(
            x_all, modsel[l], norm1_g[l].reshape(1, d), w_in_p, tab_a, tab_b, qg, wuq, kvg, wukv, n_b, s_len)

        ya = _swa_call(swa_sink[l], qa, ka, va, ctx_len)
        yb = _mla_call(qb, kb, vb, ctx_len)

        bmat, cmat, a_r, a_i = _s5_matrices(s5_lambda_re[l], s5_lambda_im[l], s5_log_step[l],
                                            s5_b_re[l], s5_b_im[l], s5_c_re[l], s5_c_im[l])
        cu_t = jnp.transpose(cu, (1, 0, 2))
        y_dirs = _s5_call(cu_t, bmat, cmat, a_r, a_i, ctx_len)
        ys_t = _s5_glu_call(cu_t.reshape(s_len * n_b, GROUP_WIDTH), y_dirs.reshape(2, s_len * n_b, GROUP_WIDTH),
                            s5_d[l].reshape(1, GROUP_WIDTH), s5_w_glu[l].astype(BF16))
        ys = jnp.transpose(ys_t.reshape(s_len, n_b, GROUP_WIDTH), (1, 0, 2))

        yd = _na_call(qd, kd, vd, _na_bias(na_rpb[l]), ctx_len)

        mixg_p, w_out_p = _mix_layout(mix_norm_g[l], w_out[l])
        x1_all, h2_all = _out_call(ya, yb, ys, yd, x_all, modsel[l], mixg_p, w_out_p, norm2_g[l].reshape(1, d))

        n_tok = n_lat if last else x_all.shape[0]
        x_all = _peer_call(h2_all, x1_all, modsel[l], peer_w_q[l].astype(BF16), peer_sub_keys[l].astype(BF16),
                           peer_u[l].astype(BF16), peer_v[l].astype(BF16).T, fg, n_tok, n_lat, l_len, last)

    return x_all[:n_lat].reshape(n_b, l_len, d)
```

```python
import functools
import math

import numpy as np
import jax
import jax.numpy as jnp
from jax import lax
from jax.experimental import pallas as pl
from jax.experimental.pallas import tpu as pltpu

F32 = jnp.float32
BF16 = jnp.bfloat16

D_MODEL = 1024
HEAD_DIM = 64
GROUP_WIDTH = 256
GRID_W = 64
EPS = 1e-6
NEG_INF = -1e30
ROPE_BASE = 10000.0
N_HEADS = 4
A_KV_HEADS = 2
A_WINDOW = 128
A_BLOCK = 128
B_NOPE = 64
B_ROPE = 32
B_Q_LORA = 192
B_KV_LORA = 128
C_GROUP = 16
C_NGROUPS = 16
C_STATE = 64
NA_KH = 8
NA_KW = 16
PEER_HEADS = 8
PEER_TOPK = 16

LANES = 128
HEAD_SLOT = LANES
TOKEN_TILE = 256
VMEM_LIMIT = 56 * 1024 * 1024

N_STATE = C_NGROUPS * C_STATE


def _cparams(sem):
    return pltpu.CompilerParams(dimension_semantics=sem, vmem_limit_bytes=VMEM_LIMIT)


def _nt_dot(a, b):
    return lax.dot_general(a, b, (((1,), (1,)), ((), ())), preferred_element_type=F32)


def _dot(a, b):
    return jnp.dot(a, b, preferred_element_type=F32)


def _ada_kernel(c_ref, w_ref, b_ref, o_ref):
    c = c_ref[...]
    s = c * jax.nn.sigmoid(c)
    o_ref[0] = _dot(s.astype(BF16), w_ref[0].astype(BF16)) + b_ref[0]


def _ada_call(cc, w_ada, b_ada):
    depth, d, n6 = w_ada.shape
    tn = 1536
    return pl.pallas_call(
        _ada_kernel,
        grid=(depth, n6 // tn),
        in_specs=[pl.BlockSpec((cc.shape[0], d), lambda l, j: (0, 0)),
                  pl.BlockSpec((1, d, tn), lambda l, j: (l, 0, j)),
                  pl.BlockSpec((1, 1, tn), lambda l, j: (l, 0, j))],
        out_specs=pl.BlockSpec((1, cc.shape[0], tn), lambda l, j: (l, 0, j)),
        out_shape=jax.ShapeDtypeStruct((depth, cc.shape[0], n6), F32),
        compiler_params=_cparams(("arbitrary", "arbitrary")),
        name="ada_mod",
    )(cc, w_ada, b_ada.reshape(depth, 1, n6))


_C_AQ = 0
_C_AK = _C_AQ + 4 * HEAD_SLOT
_C_AV = _C_AK + 2 * HEAD_SLOT
_C_BQ = _C_AV + 2 * HEAD_SLOT
_C_BKV = _C_BQ + 256
_C_BKR = _C_BKV + 128
_C_CU = _C_BKR + 128
_C_DQ = _C_CU + 256
_C_DK = _C_DQ + 4 * HEAD_SLOT
_C_DV = _C_DK + 4 * HEAD_SLOT
_C_END = _C_DV + 4 * HEAD_SLOT


def _rms(x, n):
    return x * lax.rsqrt(jnp.sum(x * x, axis=-1, keepdims=True) * (1.0 / n) + EPS)


def _rope(t, tab_ref, shift):
    return (t * tab_ref[0]
            + pltpu.roll(t, shift, 1) * tab_ref[1]
            + pltpu.roll(t, LANES - shift, 1) * tab_ref[2])


def _proj_kernel(x_ref, mod_ref, g_ref, w_ref, ta_ref, tb_ref, qg_ref, wuq_ref, kvg_ref, wukv_ref,
                 qa_ref, ka_ref, va_ref, qb_ref, kb_ref, vb_ref, cu_ref, qd_ref, kd_ref, vd_ref):
    x = x_ref[...]
    m = mod_ref[0, 0]
    h = _rms(x, D_MODEL) * g_ref[...]
    h = h * (1.0 + m[1:2]) + m[0:1]
    z = _dot(h.astype(BF16), w_ref[...])

    a_scale = HEAD_DIM ** -0.5
    for hh in range(N_HEADS):
        c0 = _C_AQ + hh * HEAD_SLOT
        qa_ref[0, :, hh * HEAD_SLOT:(hh + 1) * HEAD_SLOT] = (
            _rope(z[:, c0:c0 + HEAD_SLOT], ta_ref, 16) * a_scale).astype(BF16)
    for g in range(A_KV_HEADS):
        c0 = _C_AK + g * HEAD_SLOT
        ka_ref[0, :, g * HEAD_SLOT:(g + 1) * HEAD_SLOT] = _rope(z[:, c0:c0 + HEAD_SLOT], ta_ref, 16).astype(BF16)
    va_ref[0] = z[:, _C_AV:_C_AV + 2 * HEAD_SLOT].astype(BF16)

    cq = _rms(z[:, _C_BQ:_C_BQ + 256], B_Q_LORA) * qg_ref[...]
    qb = _dot(cq.astype(BF16), wuq_ref[...])
    ckv = _rms(z[:, _C_BKV:_C_BKV + 128], B_KV_LORA) * kvg_ref[...]
    kv = _dot(ckv.astype(BF16), wukv_ref[...])
    kr = _rope(z[:, _C_BKR:_C_BKR + 128], tb_ref, 8)
    lane = lax.broadcasted_iota(jnp.int32, (1, HEAD_SLOT), 1)
    ones_col = (lane == HEAD_DIM).astype(F32)
    for hh in range(N_HEADS):
        sl = slice(hh * HEAD_SLOT, (hh + 1) * HEAD_SLOT)
        qb_ref[0, :, sl] = _rope(qb[:, sl], tb_ref, 8).astype(BF16)
        kb_ref[0, :, sl] = (kv[:, sl] + kr).astype(BF16)
        vsl = slice(4 * HEAD_SLOT + hh * HEAD_SLOT, 4 * HEAD_SLOT + (hh + 1) * HEAD_SLOT)
        vb_ref[0, :, sl] = (kv[:, vsl] + ones_col).astype(BF16)

    cu_ref[0] = z[:, _C_CU:_C_CU + 256]
    qd_ref[0] = (z[:, _C_DQ:_C_DQ + 4 * HEAD_SLOT] * a_scale).astype(BF16)
    kd_ref[0] = z[:, _C_DK:_C_DK + 4 * HEAD_SLOT].astype(BF16)
    vd_ref[0] = z[:, _C_DV:_C_DV + 4 * HEAD_SLOT].astype(BF16)


def _stream_block(b, t, n_b, lat_tiles):
    return jnp.where(t == 0, n_b * lat_tiles + b, b * lat_tiles + t - 1)


def _proj_call(x_all, modsel, g1, w_in_p, tab_a, tab_b, qg, wuq, kvg, wukv, n_b, s_len):
    tiles = s_len // TOKEN_TILE
    lat_tiles = tiles - 1
    T = TOKEN_TILE
    xmap = lambda b, t: (_stream_block(b, t, n_b, lat_tiles), 0)
    full = lambda shape: pl.BlockSpec(shape, lambda b, t: (0,) * len(shape))
    omap = lambda b, t: (b, t, 0)
    widths = [4 * HEAD_SLOT, 2 * HEAD_SLOT, 2 * HEAD_SLOT, 4 * HEAD_SLOT, 4 * HEAD_SLOT, 4 * HEAD_SLOT,
              256, 4 * HEAD_SLOT, 4 * HEAD_SLOT, 4 * HEAD_SLOT]
    dts = [BF16, BF16, BF16, BF16, BF16, BF16, F32, BF16, BF16, BF16]
    return pl.pallas_call(
        _proj_kernel,
        grid=(n_b, tiles),
        in_specs=[pl.BlockSpec((T, D_MODEL), xmap),
                  pl.BlockSpec((1, 1, 6, D_MODEL), lambda b, t: (b, jnp.minimum(t, 1), 0, 0)),
                  full((1, D_MODEL)),
                  full(w_in_p.shape),
                  pl.BlockSpec((3, T, LANES), lambda b, t: (0, t, 0)),
                  pl.BlockSpec((3, T, LANES), lambda b, t: (0, t, 0)),
                  full(qg.shape), full(wuq.shape), full(kvg.shape), full(wukv.shape)],
        out_specs=[pl.BlockSpec((1, T, w), omap) for w in widths],
        out_shape=[jax.ShapeDtypeStruct((n_b, s_len, w), dt) for w, dt in zip(widths, dts)],
        compiler_params=_cparams(("arbitrary", "arbitrary")),
        name="mod_proj",
    )(x_all, modsel, g1, w_in_p, tab_a, tab_b, qg, wuq, kvg, wukv)


def _swa_kernel(sink_ref, q_ref, k_ref, v_ref, o_ref, *, ctx_len, s_len):
    n = pl.program_id(1)
    is_lat = n >= ctx_len // A_BLOCK
    band = 3 * A_BLOCK
    ks = pl.multiple_of(jnp.clip((n - 1) * A_BLOCK, 0, s_len - band), A_BLOCK)
    q = q_ref[0]
    kl = k_ref[0, pl.ds(ks, band), :]
    vl = v_ref[0, pl.ds(ks, band), :]
    kc = k_ref[0, 0:ctx_len, :]
    vc = v_ref[0, 0:ctx_len, :]
    qpos = n * A_BLOCK + lax.broadcasted_iota(jnp.int32, (A_BLOCK, band), 0)
    kpos = ks + lax.broadcasted_iota(jnp.int32, (A_BLOCK, band), 1)
    window = jnp.where(is_lat, A_WINDOW, -1)
    valid = (jnp.abs(qpos - kpos) <= window) & (kpos >= ctx_len)
    rep = N_HEADS // A_KV_HEADS
    scores = []
    for hh in range(N_HEADS):
        qs = slice(hh * HEAD_SLOT, (hh + 1) * HEAD_SLOT)
        gs = slice((hh // rep) * HEAD_SLOT, (hh // rep + 1) * HEAD_SLOT)
        scores.append((_nt_dot(q[:, qs], kl[:, gs]), _nt_dot(q[:, qs], kc[:, gs])))
    for hh in range(N_HEADS):
        g = hh // rep
        qs = slice(hh * HEAD_SLOT, (hh + 1) * HEAD_SLOT)
        gs = slice(g * HEAD_SLOT, (g + 1) * HEAD_SLOT)
        s_loc = jnp.where(valid, scores[hh][0], NEG_INF)
        s_ctx = scores[hh][1]
        sk = sink_ref[hh]
        mx = jnp.maximum(jnp.max(s_loc, axis=-1, keepdims=True), jnp.max(s_ctx, axis=-1, keepdims=True))
        mx = jnp.maximum(mx, sk)
        p_loc = jnp.exp(s_loc - mx)
        p_ctx = jnp.exp(s_ctx - mx)
        den = (jnp.sum(p_loc, axis=-1, keepdims=True) + jnp.sum(p_ctx, axis=-1, keepdims=True)
               + jnp.exp(sk - mx))
        o = _dot(p_loc.astype(BF16), vl[:, gs]) + _dot(p_ctx.astype(BF16), vc[:, gs])
        o_ref[0, :, qs] = o * (1.0 / den)


def _swa_call(sink, qa, ka, va, ctx_len):
    n_b, s_len, _ = qa.shape
    kern = functools.partial(_swa_kernel, ctx_len=ctx_len, s_len=s_len)
    return pl.pallas_call(
        kern,
        grid=(n_b, s_len // A_BLOCK),
        in_specs=[pl.BlockSpec(memory_space=pltpu.SMEM),
                  pl.BlockSpec((1, A_BLOCK, 4 * HEAD_SLOT), lambda b, n: (b, n, 0)),
                  pl.BlockSpec((1, s_len, 2 * HEAD_SLOT), lambda b, n: (b, 0, 0)),
                  pl.BlockSpec((1, s_len, 2 * HEAD_SLOT), lambda b, n: (b, 0, 0))],
        out_specs=pl.BlockSpec((1, A_BLOCK, 4 * HEAD_SLOT), lambda b, n: (b, n, 0)),
        out_shape=jax.ShapeDtypeStruct((n_b, s_len, 4 * HEAD_SLOT), F32),
        compiler_params=_cparams(("arbitrary", "arbitrary")),
        name="swa_mixer",
    )(sink, qa, ka, va)


MLA_TQ = 256
MLA_CK = 512


def _mla_kernel(q_ref, k_ref, v_ref, o_ref, *, ctx_len, s_len):
    t = pl.program_id(1)
    is_lat = t >= ctx_len // MLA_TQ
    scale = (B_NOPE + B_ROPE) ** -0.5
    lane = lax.broadcasted_iota(jnp.int32, (1, HEAD_SLOT), 1)

    def attend(n_keys):
        def scores(hh):
            sl = slice(hh * HEAD_SLOT, (hh + 1) * HEAD_SLOT)
            return _nt_dot(q_ref[0, :, sl], k_ref[0, 0:n_keys, sl]) * scale

        s_next = scores(0)
        for hh in range(N_HEADS):
            sl = slice(hh * HEAD_SLOT, (hh + 1) * HEAD_SLOT)
            s = s_next
            if hh + 1 < N_HEADS:
                s_next = scores(hh + 1)
            p = jnp.exp(s - jnp.max(s, axis=-1, keepdims=True))
            acc = _dot(p.astype(BF16), v_ref[0, 0:n_keys, sl])
            den = acc[:, HEAD_DIM:HEAD_DIM + 1]
            o_ref[0, :, sl] = jnp.where(lane < HEAD_DIM, acc * (1.0 / den), 0.0)

    @pl.when(is_lat)
    def _():
        attend(s_len)

    @pl.when(jnp.logical_not(is_lat))
    def _():
        attend(ctx_len)


def _mla_call(qb, kb, vb, ctx_len):
    n_b, s_len, _ = qb.shape
    kern = functools.partial(_mla_kernel, ctx_len=ctx_len, s_len=s_len)
    return pl.pallas_call(
        kern,
        grid=(n_b, s_len // MLA_TQ),
        in_specs=[pl.BlockSpec((1, MLA_TQ, 4 * HEAD_SLOT), lambda b, t: (b, t, 0)),
                  pl.BlockSpec((1, s_len, 4 * HEAD_SLOT), lambda b, t: (b, 0, 0)),
                  pl.BlockSpec((1, s_len, 4 * HEAD_SLOT), lambda b, t: (b, 0, 0))],
        out_specs=pl.BlockSpec((1, MLA_TQ, 4 * HEAD_SLOT), lambda b, t: (b, t, 0)),
        out_shape=jax.ShapeDtypeStruct((n_b, s_len, 4 * HEAD_SLOT), F32),
        compiler_params=_cparams(("arbitrary", "arbitrary")),
        name="mla_mixer",
    )(qb, kb, vb)


S5_T = 128


def _s5_kernel(u_ref, bm_ref, cm_ref, ar_ref, ai_ref, y_ref, hr_scr, hi_scr, bu_scr, *, n_b):
    d = pl.program_id(0)
    c = pl.program_id(1)

    @pl.when(c == 0)
    def _():
        hr_scr[...] = jnp.zeros_like(hr_scr)
        hi_scr[...] = jnp.zeros_like(hi_scr)

    u = u_ref[...].reshape(S5_T * n_b, GROUP_WIDTH)
    bu_scr[...] = _dot(u.astype(BF16), bm_ref[0])
    a_r = jnp.broadcast_to(ar_ref[0], (n_b, N_STATE))
    a_i = jnp.broadcast_to(ai_ref[0], (n_b, N_STATE))

    def step(i, carry):
        h_r, h_i = carry
        t = jnp.where(d == 0, i, S5_T - 1 - i)
        row = pl.multiple_of(t * n_b, n_b)
        b_r = bu_scr[pl.ds(row, n_b), 0:N_STATE]
        b_i = bu_scr[pl.ds(row, n_b), N_STATE:2 * N_STATE]
        n_r = a_r * h_r - a_i * h_i + b_r
        n_i = a_r * h_i + a_i * h_r + b_i
        bu_scr[pl.ds(row, n_b), 0:N_STATE] = n_r
        bu_scr[pl.ds(row, n_b), N_STATE:2 * N_STATE] = n_i
        return n_r, n_i

    h_r, h_i = lax.fori_loop(0, S5_T, step, (hr_scr[...], hi_scr[...]), unroll=4)
    hr_scr[...] = h_r
    hi_scr[...] = h_i
    y = _dot(bu_scr[...].astype(BF16), cm_ref[0])
    y_ref[0] = y.reshape(S5_T, n_b, GROUP_WIDTH)


def _s5_chunk(d, c, n_chunks, ctx_chunks):
    rev = jnp.where(c < ctx_chunks, ctx_chunks - 1 - c, n_chunks - 1 - (c - ctx_chunks))
    return jnp.where(d == 0, c, rev)


def _s5_call(cu_t, bmat, cmat, a_r, a_i, ctx_len):
    s_len, n_b, _ = cu_t.shape
    n_chunks = s_len // S5_T
    ctx_chunks = ctx_len // S5_T
    cmap = lambda d, c: (_s5_chunk(d, c, n_chunks, ctx_chunks), 0, 0)
    kern = functools.partial(_s5_kernel, n_b=n_b)
    return pl.pallas_call(
        kern,
        grid=(2, n_chunks),
        in_specs=[pl.BlockSpec((S5_T, n_b, GROUP_WIDTH), cmap),
                  pl.BlockSpec((1, GROUP_WIDTH, 2 * N_STATE), lambda d, c: (d, 0, 0)),
                  pl.BlockSpec((1, 2 * N_STATE, GROUP_WIDTH), lambda d, c: (d, 0, 0)),
                  pl.BlockSpec((1, 1, N_STATE), lambda d, c: (d, 0, 0)),
                  pl.BlockSpec((1, 1, N_STATE), lambda d, c: (d, 0, 0))],
        out_specs=pl.BlockSpec((1, S5_T, n_b, GROUP_WIDTH),
                               lambda d, c: (d, _s5_chunk(d, c, n_chunks, ctx_chunks), 0, 0)),
        out_shape=jax.ShapeDtypeStruct((2, s_len, n_b, GROUP_WIDTH), F32),
        scratch_shapes=[pltpu.VMEM((n_b, N_STATE), F32), pltpu.VMEM((n_b, N_STATE), F32),
                        pltpu.VMEM((S5_T * n_b, 2 * N_STATE), F32)],
        compiler_params=_cparams(("arbitrary", "arbitrary")),
        name="s5_scan",
    )(cu_t, bmat, cmat, a_r, a_i)


def _s5_glu_kernel(u_ref, y_ref, d_ref, w_ref, o_ref):
    y = u_ref[...] * d_ref[...] + y_ref[0] + y_ref[1]
    z = _dot(jax.nn.gelu(y).astype(BF16), w_ref[...])
    o_ref[...] = z[:, :GROUP_WIDTH] * jax.nn.sigmoid(z[:, GROUP_WIDTH:])


def _s5_glu_call(cu_flat, y_dirs, d_skip, w_glu):
    n = cu_flat.shape[0]
    tr = 1024
    return pl.pallas_call(
        _s5_glu_kernel,
        grid=(n // tr,),
        in_specs=[pl.BlockSpec((tr, GROUP_WIDTH), lambda i: (i, 0)),
                  pl.BlockSpec((2, tr, GROUP_WIDTH), lambda i: (0, i, 0)),
                  pl.BlockSpec((1, GROUP_WIDTH), lambda i: (0, 0)),
                  pl.BlockSpec((GROUP_WIDTH, 2 * GROUP_WIDTH), lambda i: (0, 0))],
        out_specs=pl.BlockSpec((tr, GROUP_WIDTH), lambda i: (i, 0)),
        out_shape=jax.ShapeDtypeStruct((n, GROUP_WIDTH), F32),
        compiler_params=_cparams(("arbitrary",)),
        name="s5_glu",
    )(cu_flat, y_dirs, d_skip, w_glu)


NA_KEYS = NA_KH * GRID_W


NA_RB = 4


def _na_kernel(q_ref, k_ref, v_ref, bias_ref, o_ref, *, ctx_len, n_rows):
    j = pl.program_id(1)
    ctx_steps = ctx_len // (NA_RB * GRID_W)
    kc = k_ref[0, 0:ctx_len, :]
    vc = v_ref[0, 0:ctx_len, :]
    scores = []
    for i in range(NA_RB):
        r = (j - ctx_steps) * NA_RB + i
        r0 = jnp.clip(r - NA_KH // 2, 0, n_rows - NA_KH)
        var = jnp.where(j < ctx_steps, NA_KH, r - r0)
        ks = pl.multiple_of(ctx_len + r0 * GRID_W, GRID_W)
        q = q_ref[0, i * GRID_W:(i + 1) * GRID_W, :]
        kl = k_ref[0, pl.ds(ks, NA_KEYS), :]
        for hh in range(N_HEADS):
            sl = slice(hh * HEAD_SLOT, (hh + 1) * HEAD_SLOT)
            scores.append((_nt_dot(q[:, sl], kl[:, sl]) + bias_ref[var, hh], _nt_dot(q[:, sl], kc[:, sl]), ks))
    for i in range(NA_RB):
        qs = slice(i * GRID_W, (i + 1) * GRID_W)
        for hh in range(N_HEADS):
            sl = slice(hh * HEAD_SLOT, (hh + 1) * HEAD_SLOT)
            s_loc, s_ctx, ks = scores[i * N_HEADS + hh]
            mx = jnp.maximum(jnp.max(s_loc, axis=-1, keepdims=True), jnp.max(s_ctx, axis=-1, keepdims=True))
            p_loc = jnp.exp(s_loc - mx)
            p_ctx = jnp.exp(s_ctx - mx)
            den = jnp.sum(p_loc, axis=-1, keepdims=True) + jnp.sum(p_ctx, axis=-1, keepdims=True)
            o = (_dot(p_loc.astype(BF16), v_ref[0, pl.ds(ks, NA_KEYS), sl])
                 + _dot(p_ctx.astype(BF16), vc[:, sl]))
            o_ref[0, qs, sl] = o * (1.0 / den)


def _na_call(qd, kd, vd, bias, ctx_len):
    n_b, s_len, _ = qd.shape
    n_rows = (s_len - ctx_len) // GRID_W
    blk = NA_RB * GRID_W
    assert ctx_len % blk == 0 and n_rows % NA_RB == 0
    kern = functools.partial(_na_kernel, ctx_len=ctx_len, n_rows=n_rows)
    return pl.pallas_call(
        kern,
        grid=(n_b, s_len // blk),
        in_specs=[pl.BlockSpec((1, blk, 4 * HEAD_SLOT), lambda b, j: (b, j, 0)),
                  pl.BlockSpec((1, s_len, 4 * HEAD_SLOT), lambda b, j: (b, 0, 0)),
                  pl.BlockSpec((1, s_len, 4 * HEAD_SLOT), lambda b, j: (b, 0, 0)),
                  pl.BlockSpec(bias.shape, lambda b, j: (0, 0, 0, 0))],
        out_specs=pl.BlockSpec((1, blk, 4 * HEAD_SLOT), lambda b, j: (b, j, 0)),
        out_shape=jax.ShapeDtypeStruct((n_b, s_len, 4 * HEAD_SLOT), F32),
        compiler_params=_cparams(("arbitrary", "arbitrary")),
        name="na_mixer",
    )(qd, kd, vd, bias)


def _out_kernel(ya_ref, yb_ref, ys_ref, yd_ref, x_ref, mod_ref, mg_ref, w_ref, g2_ref, x1_ref, h2_ref):
    m = mod_ref[0, 0]
    mg = mg_ref[...]
    parts = []
    off = 0
    for ref, width in ((ya_ref, 4 * HEAD_SLOT), (yb_ref, 4 * HEAD_SLOT), (ys_ref, GROUP_WIDTH),
                       (yd_ref, 4 * HEAD_SLOT)):
        parts.append((_rms(ref[0], GROUP_WIDTH) * mg[:, off:off + width]).astype(BF16))
        off += width
    y = _dot(jnp.concatenate(parts, axis=-1), w_ref[...])
    x1 = x_ref[...] + m[2:3] * y
    x1_ref[...] = x1
    h2 = _rms(x1, D_MODEL) * g2_ref[...]
    h2_ref[...] = (h2 * (1.0 + m[4:5]) + m[3:4]).astype(BF16)


def _out_call(ya, yb, ys, yd, x_all, modsel, mixg_p, w_out_p, g2):
    n_b, s_len, _ = ya.shape
    tiles = s_len // TOKEN_TILE
    lat_tiles = tiles - 1
    T = TOKEN_TILE
    xmap = lambda b, t: (_stream_block(b, t, n_b, lat_tiles), 0)
    full = lambda shape: pl.BlockSpec(shape, lambda b, t: (0,) * len(shape))
    ymap = lambda b, t: (b, t, 0)
    return pl.pallas_call(
        _out_kernel,
        grid=(n_b, tiles),
        in_specs=[pl.BlockSpec((1, T, ya.shape[-1]), ymap), pl.BlockSpec((1, T, yb.shape[-1]), ymap),
                  pl.BlockSpec((1, T, ys.shape[-1]), ymap), pl.BlockSpec((1, T, yd.shape[-1]), ymap),
                  pl.BlockSpec((T, D_MODEL), xmap),
                  pl.BlockSpec((1, 1, 6, D_MODEL), lambda b, t: (b, jnp.minimum(t, 1), 0, 0)),
                  full(mixg_p.shape), full(w_out_p.shape), full((1, D_MODEL))],
        out_specs=[pl.BlockSpec((T, D_MODEL), xmap), pl.BlockSpec((T, D_MODEL), xmap)],
        out_shape=[jax.ShapeDtypeStruct(x_all.shape, F32), jax.ShapeDtypeStruct(x_all.shape, BF16)],
        compiler_params=_cparams(("arbitrary", "arbitrary")),
        name="out_proj",
    )(ya, yb, ys, yd, x_all, modsel, mixg_p, w_out_p, g2)


PEER_TT = 512
PEER_EB = 1024
PEER_SB = 256
PEER_CW = 256
PEER_IGRP = 2
PEER_PAD = 8


_GELU_B = -2.0 * math.sqrt(2.0 / math.pi) * math.log2(math.e)
_GELU_A = 0.044715 * _GELU_B


def _gelu_tanh(x):
    e = jnp.exp2(x * (x * x * _GELU_A + _GELU_B))
    return x * (1.0 / (1.0 + e))


def _peer_kernel(h2_ref, x1_ref, mod_ref, wq_ref, sk_ref, u_ref, vt_ref, fg_ref, o_ref,
                 h2t_scr, q_scr, g_scr, n_scr, c1_scr, v1_scr, v2_scr, z_scr, p_scr, out_scr,
                 *, n_keys, final):
    e = pl.program_id(1)
    n_e = pl.num_programs(1)
    TT = PEER_TT
    LT = TT // LANES
    NK = n_keys
    e2_row0 = NK + PEER_PAD
    neg = -jnp.inf

    @pl.when(e == 0)
    def _():
        h2 = h2_ref[...]
        h2t_scr[...] = h2.astype(F32).T.astype(BF16)
        q_scr[...] = _dot(h2, wq_ref[...]).astype(BF16)
        out_scr[...] = jnp.zeros_like(out_scr)

        def extract(s, v_scr, ls, want_rank):
            w = s
            rank = jnp.full(s.shape, 127.0, F32) if want_rank else None
            for k in range(PEER_TOPK):
                mk = jnp.max(w, axis=0, keepdims=True)
                v_scr[k:k + 1, ls] = mk
                hit = w == mk
                if want_rank:
                    rank = jnp.where(hit, float(k), rank)
                if k + 1 < PEER_TOPK:
                    w = jnp.where(hit, neg, w)
            return rank

        def head(hh, carry):
            c0 = pl.multiple_of(hh * 256, 256)
            s1_all = _nt_dot(sk_ref[0], q_scr[:, pl.ds(c0, 128)])
            s2_all = _nt_dot(sk_ref[1], q_scr[:, pl.ds(c0 + 128, 128)])
            for lt in range(LT):
                ls = slice(lt * LANES, (lt + 1) * LANES)
                s1 = s1_all[:, ls]
                s2 = s2_all[:, ls]
                extract(s1, v1_scr, ls, False)
                r2 = extract(s2, v2_scr, ls, True)
                slabs = [v1_scr[0:1, ls] + v2_scr[0:16, ls]]
                for a in range(1, 5):
                    slabs.append(v1_scr[a:a + 1, ls] + v2_scr[0:8, ls])
                slabs.append(v1_scr[0:8, ls] + v2_scr[1:2, ls])
                slabs.append(v1_scr[0:8, ls] + v2_scr[0:1, ls])
                slabs.append(v1_scr[8:16, ls] + v2_scr[0:1, ls])
                cand = jnp.concatenate(slabs, axis=0)
                top = v1_scr[0:1, ls] + v2_scr[0:1, ls]
                zsum = jnp.zeros((1, LANES), F32)
                tau = top
                for k in range(PEER_TOPK):
                    tau = jnp.max(cand, axis=0, keepdims=True)
                    zsum = zsum + jnp.exp(tau - top)
                    if k + 1 < PEER_TOPK:
                        cand = jnp.where(cand == tau, neg, cand)
                count = jnp.zeros(s1.shape, F32)
                for a in range(PEER_TOPK):
                    v1a = v1_scr[a:a + 1, ls]
                    sel = (v1a + v2_scr[0:16, ls]) >= tau
                    cnt = jnp.sum(jnp.where(sel, 1.0, 0.0), axis=0, keepdims=True)
                    count = jnp.where(s1 == v1a, cnt, count)
                n_scr[hh, :, ls] = count
                c1_scr[hh, :, ls] = jnp.exp(s1 - v1_scr[0:1, ls]) * (1.0 / zsum)
                g_scr[lt, hh, 0:NK, :] = r2
                g_scr[lt, hh, e2_row0:e2_row0 + NK, :] = jnp.exp(s2 - v2_scr[0:1, ls])
            return carry

        lax.fori_loop(0, PEER_HEADS, head, 0)

    EB = u_ref.shape[0]
    i_per = EB // NK
    i_per_sb = PEER_SB // NK
    jt_n = NK // 16
    lt_per = PEER_CW // LANES
    n_half = TT // PEER_CW
    chunks = [(sb, hf) for hf in range(n_half) for sb in range(EB // PEER_SB)]
    i_grp = min(PEER_IGRP, i_per_sb)

    def stage_z(c):
        sb, hf = chunks[c]
        rows = slice(sb * PEER_SB, (sb + 1) * PEER_SB)
        z = _dot(u_ref[rows, :], h2t_scr[:, hf * PEER_CW:(hf + 1) * PEER_CW])
        for l in range(lt_per):
            z_scr[hf * lt_per + l, rows, :] = z[:, l * LANES:(l + 1) * LANES]

    def stage_gate(c):
        sb, hf = chunks[c]
        for lt in range(hf * lt_per, (hf + 1) * lt_per):
            ls = slice(lt * LANES, (lt + 1) * LANES)
            for g0 in range(0, i_per_sb, i_grp):
                nrow, crow = [], []
                for i2 in range(g0, g0 + i_grp):
                    ii = sb * i_per_sb + i2
                    i8 = pl.multiple_of(e * i_per + (ii // 8) * 8, 8)
                    r8 = slice(ii % 8, ii % 8 + 1)
                    nrow.append([jnp.broadcast_to(n_scr[hh, pl.ds(i8, 8), ls][r8], (8, LANES))
                                 for hh in range(PEER_HEADS)])
                    crow.append([jnp.broadcast_to(c1_scr[hh, pl.ds(i8, 8), ls][r8], (8, LANES))
                                 for hh in range(PEER_HEADS)])
                for jt in range(jt_n):
                    halves = [[None, None] for _ in range(i_grp)]
                    for s8 in range(2):
                        j0 = jt * 16 + s8 * 8
                        w = [None] * i_grp
                        for hh in range(PEER_HEADS):
                            r2t = g_scr[lt, hh, j0:j0 + 8, :]
                            e2t = g_scr[lt, hh, e2_row0 + j0:e2_row0 + j0 + 8, :]
                            for k in range(i_grp):
                                term = jnp.where(r2t < nrow[k][hh], e2t, 0.0) * crow[k][hh]
                                w[k] = term if w[k] is None else w[k] + term
                        for k in range(i_grp):
                            r0 = sb * PEER_SB + (g0 + k) * NK + j0
                            halves[k][s8] = w[k] * _gelu_tanh(z_scr[lt, r0:r0 + 8, :])
                    for k in range(i_grp):
                        r0 = sb * PEER_SB + (g0 + k) * NK + jt * 16
                        p_scr[lt, r0:r0 + 16, :] = jnp.concatenate(halves[k], axis=0).astype(BF16)

    def stage_out(hf):
        p_h = jnp.concatenate([p_scr[hf * lt_per + l] for l in range(lt_per)], axis=1)
        cs = slice(hf * PEER_CW, (hf + 1) * PEER_CW)
        out_scr[:, cs] += _dot(vt_ref[0], p_h)

    for c in range(len(chunks)):
        stage_z(c)
    for c in range(len(chunks)):
        stage_gate(c)
        if (c + 1) % (EB // PEER_SB) == 0:
            stage_out(chunks[c][1])

    @pl.when(e == n_e - 1)
    def _():
        x2 = x1_ref[...] + mod_ref[0, 0][5:6] * out_scr[...].T
        if final:
            x2 = _rms(x2, D_MODEL) * fg_ref[...]
        o_ref[...] = x2


def _peer_call(h2_all, x1_all, modsel, wq, subk, u_tab, v_tab, fg, n_tok, n_lat, l_len, final):
    n_exp = u_tab.shape[0]
    n_keys = subk.shape[1]
    assert n_exp == n_keys * n_keys and n_keys % 16 == 0
    assert min(PEER_EB, n_exp) % (8 * n_keys) == 0
    assert PEER_SB % n_keys == 0 and min(PEER_EB, n_exp) % PEER_SB == 0
    assert n_tok % PEER_TT == 0 and n_lat % PEER_TT == 0 and l_len % PEER_TT == 0
    TT, EB = PEER_TT, min(PEER_EB, n_exp)
    lat_tiles = n_lat // TT
    per_b = l_len // TT

    def mod_map(i, e):
        return (jnp.where(i < lat_tiles, i // per_b, 0), (i < lat_tiles).astype(jnp.int32), 0, 0)

    n_blk = n_exp // EB
    vt_tab = jnp.transpose(v_tab.reshape(n_blk, EB, D_MODEL), (0, 2, 1))
    kern = functools.partial(_peer_kernel, n_keys=n_keys, final=final)
    full = lambda shape: pl.BlockSpec(shape, lambda i, e: (0,) * len(shape))
    return pl.pallas_call(
        kern,
        grid=(n_tok // TT, n_blk),
        in_specs=[pl.BlockSpec((TT, D_MODEL), lambda i, e: (i, 0)),
                  pl.BlockSpec((TT, D_MODEL), lambda i, e: (i, 0)),
                  pl.BlockSpec((1, 1, 6, D_MODEL), mod_map),
                  full(wq.shape), full(subk.shape),
                  pl.BlockSpec((EB, D_MODEL), lambda i, e: (e, 0)),
                  pl.BlockSpec((1, D_MODEL, EB), lambda i, e: (e, 0, 0)),
                  full((1, D_MODEL))],
        out_specs=pl.BlockSpec((TT, D_MODEL), lambda i, e: (i, 0)),
        out_shape=jax.ShapeDtypeStruct((n_tok, D_MODEL), F32),
        scratch_shapes=[pltpu.VMEM((D_MODEL, TT), BF16),
                        pltpu.VMEM((TT, PEER_HEADS * 256), BF16),
                        pltpu.VMEM((TT // LANES, PEER_HEADS, 2 * (n_keys + PEER_PAD), LANES), F32),
                        pltpu.VMEM((PEER_HEADS, n_keys, TT), F32),
                        pltpu.VMEM((PEER_HEADS, n_keys, TT), F32),
                        pltpu.VMEM((PEER_TOPK, TT), F32),
                        pltpu.VMEM((PEER_TOPK, TT), F32),
                        pltpu.VMEM((TT // LANES, EB, LANES), F32),
                        pltpu.VMEM((TT // LANES, EB, LANES), BF16),
                        pltpu.VMEM((D_MODEL, TT), F32)],
        compiler_params=_cparams(("arbitrary", "arbitrary")),
        name="peer_ffn",
    )(h2_all, x1_all, modsel, wq, subk, u_tab, vt_tab, fg)


def _head_slots(w, n_heads, width=HEAD_DIM):
    lead = w.shape[:-1]
    w = w.reshape(lead + (n_heads, width))
    w = jnp.pad(w, [(0, 0)] * len(lead) + [(0, 0), (0, HEAD_SLOT - width)])
    return w.reshape(lead + (n_heads * HEAD_SLOT,))


def _prep_w_in(w_in):
    sizes = (256, 128, 128, B_Q_LORA, B_KV_LORA, B_ROPE, 256, 256, 256, 256)
    offs = np.cumsum((0,) + sizes)
    aq, ak, av, bq, bkv, bkr, cu, dq, dk, dv = [w_in[:, offs[i]:offs[i + 1]] for i in range(10)]
    d = w_in.shape[0]
    bkr_p = jnp.concatenate([jnp.zeros((d, B_NOPE), F32), bkr, jnp.zeros((d, HEAD_SLOT - B_NOPE - B_ROPE), F32)], 1)
    cols = [_head_slots(aq, 4), _head_slots(ak, 2), _head_slots(av, 2),
            jnp.pad(bq, ((0, 0), (0, 256 - B_Q_LORA))), bkv, bkr_p, cu,
            _head_slots(dq, 4), _head_slots(dk, 4), _head_slots(dv, 4)]
    w = jnp.concatenate(cols, axis=1)
    assert w.shape[1] == _C_END
    return w.astype(BF16)


def _prep_mla(w_uq, w_ukv, qg, kvg):
    wq = _head_slots(w_uq, N_HEADS, B_NOPE + B_ROPE)
    wq = jnp.pad(wq, ((0, 256 - B_Q_LORA), (0, 0))).astype(BF16)
    kv = w_ukv.reshape(B_KV_LORA, N_HEADS, B_NOPE + HEAD_DIM)
    wk = _head_slots(kv[:, :, :B_NOPE].reshape(B_KV_LORA, -1), N_HEADS, B_NOPE)
    wv = _head_slots(kv[:, :, B_NOPE:].reshape(B_KV_LORA, -1), N_HEADS, HEAD_DIM)
    wkv = jnp.concatenate([wk, wv], axis=1).astype(BF16)
    qg_p = jnp.pad(qg, (0, 256 - B_Q_LORA)).reshape(1, 256)
    return wq, wkv, qg_p, kvg.reshape(1, B_KV_LORA)


def _rope_tables(ctx_len, l_len):
    t = np.arange(l_len)
    pos = np.stack([t // GRID_W, t % GRID_W], 0).astype(np.float64)

    def build(width, lane0):
        a = width // 2
        half = a // 2
        inv = ROPE_BASE ** (-np.arange(half, dtype=np.float64) / half)
        cos = np.ones((l_len, LANES)); sp = np.zeros((l_len, LANES)); sm = np.zeros((l_len, LANES))
        for j in range(width):
            axis, i = j // a, j % a
            f, second = i % half, i >= half
            ang = pos[axis] * inv[f]
            cos[:, lane0 + j] = np.cos(ang)
            if second:
                sp[:, lane0 + j] = np.sin(ang)
            else:
                sm[:, lane0 + j] = -np.sin(ang)
        tab = np.stack([cos, sp, sm], 0)
        ctx = np.stack([np.ones((ctx_len, LANES)), np.zeros((ctx_len, LANES)), np.zeros((ctx_len, LANES))], 0)
        return jnp.asarray(np.concatenate([ctx, tab], axis=1), F32)

    return build(HEAD_DIM, 0), build(B_ROPE, B_NOPE)


def _s5_matrices(lam_re, lam_im, log_step, b_re, b_im, c_re, c_im):
    lr, li = lam_re.astype(F32), lam_im.astype(F32)
    step = jnp.exp(log_step.astype(F32))[..., None]
    mag = jnp.exp(lr * step)
    abr, abi = mag * jnp.cos(li * step), mag * jnp.sin(li * step)
    nr, ni = abr - 1.0, abi
    den = lr * lr + li * li
    fr = (nr * lr + ni * li) / den
    fi = (ni * lr - nr * li) / den
    bbr = fr[..., None] * b_re - fi[..., None] * b_im
    bbi = fr[..., None] * b_im + fi[..., None] * b_re
    eye = jnp.eye(C_NGROUPS, dtype=F32)
    bm_r = jnp.einsum('dgpc,gh->dgchp', bbr, eye).reshape(2, GROUP_WIDTH, N_STATE)
    bm_i = jnp.einsum('dgpc,gh->dgchp', bbi, eye).reshape(2, GROUP_WIDTH, N_STATE)
    bmat = jnp.concatenate([bm_r, bm_i], axis=2).astype(BF16)
    cm_r = jnp.einsum('dgcp,gh->dgphc', c_re.astype(F32), eye).reshape(2, N_STATE, GROUP_WIDTH)
    cm_i = jnp.einsum('dgcp,gh->dgphc', c_im.astype(F32), eye).reshape(2, N_STATE, GROUP_WIDTH)
    cmat = jnp.concatenate([cm_r, -cm_i], axis=1).astype(BF16)
    return bmat, cmat, abr.reshape(2, 1, N_STATE), abi.reshape(2, 1, N_STATE)


def _na_bias(rpb):
    w = np.arange(GRID_W)
    cs = np.clip(w - NA_KW // 2, 0, GRID_W - NA_KW)
    c = np.arange(GRID_W)
    inwin = (c[None, :] >= cs[:, None]) & (c[None, :] < cs[:, None] + NA_KW)
    colidx = np.clip(c[None, :] - w[:, None] + NA_KW - 1, 0, 2 * NA_KW - 2)
    y = np.arange(NA_KH)
    variants = []
    for d in range(NA_KH):
        rowidx = y - d + NA_KH - 1
        b = rpb.astype(F32)[:, rowidx][:, :, colidx]
        b = jnp.where(jnp.asarray(inwin)[None, None], b, NEG_INF)
        variants.append(jnp.transpose(b, (0, 2, 1, 3)).reshape(N_HEADS, GRID_W, NA_KEYS))
    variants.append(jnp.full((N_HEADS, GRID_W, NA_KEYS), NEG_INF, F32))
    return jnp.stack(variants, 0)


def _mix_layout(mix_norm_g, w_out):
    g = mix_norm_g.reshape(4, GROUP_WIDTH)
    w = w_out.reshape(4, GROUP_WIDTH, D_MODEL)
    gs, ws = [], []
    for k in range(4):
        if k == 2:
            gs.append(g[k]); ws.append(w[k])
        else:
            gs.append(_head_slots(g[k], N_HEADS))
            wk = w[k].reshape(N_HEADS, HEAD_DIM, D_MODEL)
            wk = jnp.pad(wk, ((0, 0), (0, HEAD_SLOT - HEAD_DIM), (0, 0))).reshape(N_HEADS * HEAD_SLOT, D_MODEL)
            ws.append(wk)
    return jnp.concatenate(gs).reshape(1, -1), jnp.concatenate(ws, axis=0).astype(BF16)


def kernel(x, c, ctx, c_ctx, norm1_g, norm2_g, w_ada, b_ada, w_in, swa_sink, mla_q_norm_g, mla_w_uq,
           mla_kv_norm_g, mla_w_ukv, s5_lambda_re, s5_lambda_im, s5_log_step, s5_b_re, s5_b_im, s5_c_re,
           s5_c_im, s5_d, s5_w_glu, na_rpb, mix_norm_g, w_out, peer_w_q, peer_sub_keys, peer_u, peer_v,
           final_norm_g):
    n_b, l_len, d = x.shape
    ctx_len = ctx.shape[1]
    depth = w_in.shape[0]
    s_len = ctx_len + l_len
    assert d == D_MODEL and ctx_len == TOKEN_TILE and n_b % 8 == 0 and n_b < 16
    assert l_len % MLA_CK == 0 and l_len // GRID_W >= NA_KH and l_len % PEER_TT == 0
    n_lat = n_b * l_len

    cc = jnp.concatenate([c, c_ctx[None], jnp.zeros((16 - n_b - 1, d), F32)], axis=0)
    mod = _ada_call(cc, w_ada, b_ada).reshape(depth, 16, 6, d)
    lat_mod = mod[:, :n_b]
    ctx_mod = jnp.broadcast_to(mod[:, n_b:n_b + 1], lat_mod.shape)
    modsel = jnp.stack([ctx_mod, lat_mod], axis=2)

    tab_a, tab_b = _rope_tables(ctx_len, l_len)
    x_all = jnp.concatenate([x.reshape(n_lat, d), ctx.reshape(n_b * ctx_len, d)], axis=0)
    fg = final_norm_g.reshape(1, d)

    for l in range(depth):
        last = l == depth - 1
        w_in_p = _prep_w_in(w_in[l])
        wuq, wukv, qg, kvg = _prep_mla(mla_w_uq[l], mla_w_ukv[l], mla_q_norm_g[l], mla_kv_norm_g[l])
        qa, ka, va, qb, kb, vb, cu, qd, kd, vd = _proj_call(
            x_all, modsel[l], norm1_g[l].reshape(1, d), w_in_p, tab_a, tab_b, qg, wuq, kvg, wukv, n_b, s_len)

        ya = _swa_call(swa_sink[l], qa, ka, va, ctx_len)
        yb = _mla_call(qb, kb, vb, ctx_len)

        bmat, cmat, a_r, a_i = _s5_matrices(s5_lambda_re[l], s5_lambda_im[l], s5_log_step[l],
                                            s5_b_re[l], s5_b_im[l], s5_c_re[l], s5_c_im[l])
        cu_t = jnp.transpose(cu, (1, 0, 2))
        y_dirs = _s5_call(cu_t, bmat, cmat, a_r, a_i, ctx_len)
        ys_t = _s5_glu_call(cu_t.reshape(s_len * n_b, GROUP_WIDTH), y_dirs.reshape(2, s_len * n_b, GROUP_WIDTH),
                            s5_d[l].reshape(1, GROUP_WIDTH), s5_w_glu[l].astype(BF16))
        ys = jnp.transpose(ys_t.reshape(s_len, n_b, GROUP_WIDTH), (1, 0, 2))

        yd = _na_call(qd, kd, vd, _na_bias(na_rpb[l]), ctx_len)

        mixg_p, w_out_p = _mix_layout(mix_norm_g[l], w_out[l])
        x1_all, h2_all = _out_call(ya, yb, ys, yd, x_all, modsel[l], mixg_p, w_out_p, norm2_g[l].reshape(1, d))

        n_tok = n_lat if last else x_all.shape[0]
        x_all = _peer_call(h2_all, x1_all, modsel[l], peer_w_q[l].astype(BF16), peer_sub_keys[l].astype(BF16),
                           peer_u[l].astype(BF16), peer_v[l].astype(BF16), fg, n_tok, n_lat, l_len, last)

    return x_all[:n_lat].reshape(n_b, l_len, d)
```

```python
import functools
import math

import numpy as np
import jax
import jax.numpy as jnp
from jax import lax
from jax.experimental import pallas as pl
from jax.experimental.pallas import tpu as pltpu

F32 = jnp.float32
BF16 = jnp.bfloat16

D_MODEL = 1024
HEAD_DIM = 64
GROUP_WIDTH = 256
GRID_W = 64
EPS = 1e-6
NEG_INF = -1e30
ROPE_BASE = 10000.0
N_HEADS = 4
A_KV_HEADS = 2
A_WINDOW = 128
A_BLOCK = 128
B_NOPE = 64
B_ROPE = 32
B_Q_LORA = 192
B_KV_LORA = 128
C_GROUP = 16
C_NGROUPS = 16
C_STATE = 64
NA_KH = 8
NA_KW = 16
PEER_HEADS = 8
PEER_TOPK = 16

LANES = 128
HEAD_SLOT = LANES
TOKEN_TILE = 256
VMEM_LIMIT = 56 * 1024 * 1024

N_STATE = C_NGROUPS * C_STATE


def _cparams(sem):
    return pltpu.CompilerParams(dimension_semantics=sem, vmem_limit_bytes=VMEM_LIMIT)


def _nt_dot(a, b):
    return lax.dot_general(a, b, (((1,), (1,)), ((), ())), preferred_element_type=F32)


def _dot(a, b):
    return jnp.dot(a, b, preferred_element_type=F32)


def _ada_kernel(c_ref, w_ref, b_ref, o_ref):
    c = c_ref[...]
    s = c * jax.nn.sigmoid(c)
    o_ref[0] = _dot(s.astype(BF16), w_ref[0].astype(BF16)) + b_ref[0]


def _ada_call(cc, w_ada, b_ada):
    depth, d, n6 = w_ada.shape
    tn = 1536
    return pl.pallas_call(
        _ada_kernel,
        grid=(depth, n6 // tn),
        in_specs=[pl.BlockSpec((cc.shape[0], d), lambda l, j: (0, 0)),
                  pl.BlockSpec((1, d, tn), lambda l, j: (l, 0, j)),
                  pl.BlockSpec((1, 1, tn), lambda l, j: (l, 0, j))],
        out_specs=pl.BlockSpec((1, cc.shape[0], tn), lambda l, j: (l, 0, j)),
        out_shape=jax.ShapeDtypeStruct((depth, cc.shape[0], n6), F32),
        compiler_params=_cparams(("arbitrary", "arbitrary")),
        name="ada_mod",
    )(cc, w_ada, b_ada.reshape(depth, 1, n6))


_C_AQ = 0
_C_AK = _C_AQ + 4 * HEAD_SLOT
_C_AV = _C_AK + 2 * HEAD_SLOT
_C_BQ = _C_AV + 2 * HEAD_SLOT
_C_BKV = _C_BQ + 256
_C_BKR = _C_BKV + 128
_C_CU = _C_BKR + 128
_C_DQ = _C_CU + 256
_C_DK = _C_DQ + 4 * HEAD_SLOT
_C_DV = _C_DK + 4 * HEAD_SLOT
_C_END = _C_DV + 4 * HEAD_SLOT


def _rms(x, n):
    return x * lax.rsqrt(jnp.sum(x * x, axis=-1, keepdims=True) * (1.0 / n) + EPS)


def _rope(t, tab_ref, shift):
    return (t * tab_ref[0]
            + pltpu.roll(t, shift, 1) * tab_ref[1]
            + pltpu.roll(t, LANES - shift, 1) * tab_ref[2])


def _proj_kernel(x_ref, mod_ref, g_ref, w_ref, ta_ref, tb_ref, qg_ref, wuq_ref, kvg_ref, wukv_ref,
                 qa_ref, ka_ref, va_ref, qb_ref, kb_ref, vb_ref, cu_ref, qd_ref, kd_ref, vd_ref):
    x = x_ref[...]
    m = mod_ref[0, 0]
    h = _rms(x, D_MODEL) * g_ref[...]
    h = h * (1.0 + m[1:2]) + m[0:1]
    z = _dot(h.astype(BF16), w_ref[...])

    a_scale = HEAD_DIM ** -0.5
    for hh in range(N_HEADS):
        c0 = _C_AQ + hh * HEAD_SLOT
        qa_ref[0, :, hh * HEAD_SLOT:(hh + 1) * HEAD_SLOT] = (
            _rope(z[:, c0:c0 + HEAD_SLOT], ta_ref, 16) * a_scale).astype(BF16)
    for g in range(A_KV_HEADS):
        c0 = _C_AK + g * HEAD_SLOT
        ka_ref[0, :, g * HEAD_SLOT:(g + 1) * HEAD_SLOT] = _rope(z[:, c0:c0 + HEAD_SLOT], ta_ref, 16).astype(BF16)
    va_ref[0] = z[:, _C_AV:_C_AV + 2 * HEAD_SLOT].astype(BF16)

    cq = _rms(z[:, _C_BQ:_C_BQ + 256], B_Q_LORA) * qg_ref[...]
    qb = _dot(cq.astype(BF16), wuq_ref[...])
    ckv = _rms(z[:, _C_BKV:_C_BKV + 128], B_KV_LORA) * kvg_ref[...]
    kv = _dot(ckv.astype(BF16), wukv_ref[...])
    kr = _rope(z[:, _C_BKR:_C_BKR + 128], tb_ref, 8)
    lane = lax.broadcasted_iota(jnp.int32, (1, HEAD_SLOT), 1)
    ones_col = (lane == HEAD_DIM).astype(F32)
    for hh in range(N_HEADS):
        sl = slice(hh * HEAD_SLOT, (hh + 1) * HEAD_SLOT)
        qb_ref[0, :, sl] = _rope(qb[:, sl], tb_ref, 8).astype(BF16)
        kb_ref[0, :, sl] = (kv[:, sl] + kr).astype(BF16)
        vsl = slice(4 * HEAD_SLOT + hh * HEAD_SLOT, 4 * HEAD_SLOT + (hh + 1) * HEAD_SLOT)
        vb_ref[0, :, sl] = (kv[:, vsl] + ones_col).astype(BF16)

    cu_ref[...] = z[:, _C_CU:_C_CU + 256]
    qd_ref[0] = (z[:, _C_DQ:_C_DQ + 4 * HEAD_SLOT] * a_scale).astype(BF16)
    kd_ref[0] = z[:, _C_DK:_C_DK + 4 * HEAD_SLOT].astype(BF16)
    vd_ref[0] = z[:, _C_DV:_C_DV + 4 * HEAD_SLOT].astype(BF16)


def _stream_block(b, t, n_b, lat_tiles):
    return jnp.where(t == 0, n_b * lat_tiles + b, b * lat_tiles + t - 1)


def _proj_call(x_all, modsel, g1, w_in_p, tab_a, tab_b, qg, wuq, kvg, wukv, n_b, s_len):
    tiles = s_len // TOKEN_TILE
    lat_tiles = tiles - 1
    T = TOKEN_TILE
    xmap = lambda b, t: (_stream_block(b, t, n_b, lat_tiles), 0)
    full = lambda shape: pl.BlockSpec(shape, lambda b, t: (0,) * len(shape))
    omap = lambda b, t: (b, t, 0)
    widths = [4 * HEAD_SLOT, 2 * HEAD_SLOT, 2 * HEAD_SLOT, 4 * HEAD_SLOT, 4 * HEAD_SLOT, 4 * HEAD_SLOT,
              256, 4 * HEAD_SLOT, 4 * HEAD_SLOT, 4 * HEAD_SLOT]
    dts = [BF16, BF16, BF16, BF16, BF16, BF16, F32, BF16, BF16, BF16]
    return pl.pallas_call(
        _proj_kernel,
        grid=(n_b, tiles),
        in_specs=[pl.BlockSpec((T, D_MODEL), xmap),
                  pl.BlockSpec((1, 1, 6, D_MODEL), lambda b, t: (b, jnp.minimum(t, 1), 0, 0)),
                  full((1, D_MODEL)),
                  full(w_in_p.shape),
                  pl.BlockSpec((3, T, LANES), lambda b, t: (0, t, 0)),
                  pl.BlockSpec((3, T, LANES), lambda b, t: (0, t, 0)),
                  full(qg.shape), full(wuq.shape), full(kvg.shape), full(wukv.shape)],
        out_specs=[pl.BlockSpec((T, w), lambda b, t: (t, b)) if i == 6 else pl.BlockSpec((1, T, w), omap)
                   for i, w in enumerate(widths)],
        out_shape=[jax.ShapeDtypeStruct((s_len, n_b * w) if i == 6 else (n_b, s_len, w), dt)
                   for i, (w, dt) in enumerate(zip(widths, dts))],
        compiler_params=_cparams(("arbitrary", "arbitrary")),
        name="mod_proj",
    )(x_all, modsel, g1, w_in_p, tab_a, tab_b, qg, wuq, kvg, wukv)


def _swa_kernel(sink_ref, q_ref, k_ref, v_ref, o_ref, *, ctx_len, s_len):
    n = pl.program_id(1)
    is_lat = n >= ctx_len // A_BLOCK
    band = 3 * A_BLOCK
    ks = pl.multiple_of(jnp.clip((n - 1) * A_BLOCK, 0, s_len - band), A_BLOCK)
    q = q_ref[0]
    kl = k_ref[0, pl.ds(ks, band), :]
    vl = v_ref[0, pl.ds(ks, band), :]
    kc = k_ref[0, 0:ctx_len, :]
    vc = v_ref[0, 0:ctx_len, :]
    qpos = n * A_BLOCK + lax.broadcasted_iota(jnp.int32, (A_BLOCK, band), 0)
    kpos = ks + lax.broadcasted_iota(jnp.int32, (A_BLOCK, band), 1)
    window = jnp.where(is_lat, A_WINDOW, -1)
    valid = (jnp.abs(qpos - kpos) <= window) & (kpos >= ctx_len)
    rep = N_HEADS // A_KV_HEADS
    scores = []
    for hh in range(N_HEADS):
        qs = slice(hh * HEAD_SLOT, (hh + 1) * HEAD_SLOT)
        gs = slice((hh // rep) * HEAD_SLOT, (hh // rep + 1) * HEAD_SLOT)
        scores.append((_nt_dot(q[:, qs], kl[:, gs]), _nt_dot(q[:, qs], kc[:, gs])))
    for hh in range(N_HEADS):
        g = hh // rep
        qs = slice(hh * HEAD_SLOT, (hh + 1) * HEAD_SLOT)
        gs = slice(g * HEAD_SLOT, (g + 1) * HEAD_SLOT)
        s_loc = jnp.where(valid, scores[hh][0], NEG_INF)
        s_ctx = scores[hh][1]
        sk = sink_ref[hh]
        mx = jnp.maximum(jnp.max(s_loc, axis=-1, keepdims=True), jnp.max(s_ctx, axis=-1, keepdims=True))
        mx = jnp.maximum(mx, sk)
        p_loc = jnp.exp(s_loc - mx)
        p_ctx = jnp.exp(s_ctx - mx)
        den = (jnp.sum(p_loc, axis=-1, keepdims=True) + jnp.sum(p_ctx, axis=-1, keepdims=True)
               + jnp.exp(sk - mx))
        o = _dot(p_loc.astype(BF16), vl[:, gs]) + _dot(p_ctx.astype(BF16), vc[:, gs])
        o_ref[0, :, qs] = o * (1.0 / den)


def _swa_call(sink, qa, ka, va, ctx_len):
    n_b, s_len, _ = qa.shape
    kern = functools.partial(_swa_kernel, ctx_len=ctx_len, s_len=s_len)
    return pl.pallas_call(
        kern,
        grid=(n_b, s_len // A_BLOCK),
        in_specs=[pl.BlockSpec(memory_space=pltpu.SMEM),
                  pl.BlockSpec((1, A_BLOCK, 4 * HEAD_SLOT), lambda b, n: (b, n, 0)),
                  pl.BlockSpec((1, s_len, 2 * HEAD_SLOT), lambda b, n: (b, 0, 0)),
                  pl.BlockSpec((1, s_len, 2 * HEAD_SLOT), lambda b, n: (b, 0, 0))],
        out_specs=pl.BlockSpec((1, A_BLOCK, 4 * HEAD_SLOT), lambda b, n: (b, n, 0)),
        out_shape=jax.ShapeDtypeStruct((n_b, s_len, 4 * HEAD_SLOT), F32),
        compiler_params=_cparams(("arbitrary", "arbitrary")),
        name="swa_mixer",
    )(sink, qa, ka, va)


MLA_TQ = 256


def _mla_kernel(q_ref, k_ref, v_ref, o_ref, *, ctx_len, s_len):
    t = pl.program_id(1)
    is_lat = t >= ctx_len // MLA_TQ
    scale = (B_NOPE + B_ROPE) ** -0.5
    lane = lax.broadcasted_iota(jnp.int32, (1, HEAD_SLOT), 1)

    def attend(n_keys):
        def scores(hh):
            sl = slice(hh * HEAD_SLOT, (hh + 1) * HEAD_SLOT)
            return _nt_dot(q_ref[0, :, sl], k_ref[0, 0:n_keys, sl]) * scale

        s_next = scores(0)
        for hh in range(N_HEADS):
            sl = slice(hh * HEAD_SLOT, (hh + 1) * HEAD_SLOT)
            s = s_next
            if hh + 1 < N_HEADS:
                s_next = scores(hh + 1)
            p = jnp.exp(s - jnp.max(s, axis=-1, keepdims=True))
            acc = _dot(p.astype(BF16), v_ref[0, 0:n_keys, sl])
            den = acc[:, HEAD_DIM:HEAD_DIM + 1]
            o_ref[0, :, sl] = jnp.where(lane < HEAD_DIM, acc * (1.0 / den), 0.0)

    @pl.when(is_lat)
    def _():
        attend(s_len)

    @pl.when(jnp.logical_not(is_lat))
    def _():
        attend(ctx_len)


def _mla_call(qb, kb, vb, ctx_len):
    n_b, s_len, _ = qb.shape
    assert ctx_len % MLA_TQ == 0 and s_len % MLA_TQ == 0
    kern = functools.partial(_mla_kernel, ctx_len=ctx_len, s_len=s_len)
    return pl.pallas_call(
        kern,
        grid=(n_b, s_len // MLA_TQ),
        in_specs=[pl.BlockSpec((1, MLA_TQ, 4 * HEAD_SLOT), lambda b, t: (b, t, 0)),
                  pl.BlockSpec((1, s_len, 4 * HEAD_SLOT), lambda b, t: (b, 0, 0)),
                  pl.BlockSpec((1, s_len, 4 * HEAD_SLOT), lambda b, t: (b, 0, 0))],
        out_specs=pl.BlockSpec((1, MLA_TQ, 4 * HEAD_SLOT), lambda b, t: (b, t, 0)),
        out_shape=jax.ShapeDtypeStruct((n_b, s_len, 4 * HEAD_SLOT), F32),
        compiler_params=_cparams(("arbitrary", "arbitrary")),
        name="mla_mixer",
    )(qb, kb, vb)


S5_T = 128


def _s5_kernel(u_ref, bm_ref, cm_ref, ar_ref, ai_ref, y_ref, hr_scr, hi_scr, bu_scr, *, n_b):
    d = pl.program_id(0)
    c = pl.program_id(1)

    @pl.when(c == 0)
    def _():
        hr_scr[...] = jnp.zeros_like(hr_scr)
        hi_scr[...] = jnp.zeros_like(hi_scr)

    u = u_ref[...].reshape(S5_T * n_b, GROUP_WIDTH)
    bu_scr[...] = _dot(u.astype(BF16), bm_ref[0])
    a_r = jnp.broadcast_to(ar_ref[0], (n_b, N_STATE))
    a_i = jnp.broadcast_to(ai_ref[0], (n_b, N_STATE))

    def step(i, carry):
        h_r, h_i = carry
        t = jnp.where(d == 0, i, S5_T - 1 - i)
        row = pl.multiple_of(t * n_b, n_b)
        b_r = bu_scr[pl.ds(row, n_b), 0:N_STATE]
        b_i = bu_scr[pl.ds(row, n_b), N_STATE:2 * N_STATE]
        n_r = a_r * h_r - a_i * h_i + b_r
        n_i = a_r * h_i + a_i * h_r + b_i
        bu_scr[pl.ds(row, n_b), 0:N_STATE] = n_r
        bu_scr[pl.ds(row, n_b), N_STATE:2 * N_STATE] = n_i
        return n_r, n_i

    h_r, h_i = lax.fori_loop(0, S5_T, step, (hr_scr[...], hi_scr[...]), unroll=4)
    hr_scr[...] = h_r
    hi_scr[...] = h_i
    y = _dot(bu_scr[...].astype(BF16), cm_ref[0])
    y_ref[0] = y.reshape(S5_T, n_b, GROUP_WIDTH)


def _s5_chunk(d, c, n_chunks, ctx_chunks):
    rev = jnp.where(c < ctx_chunks, ctx_chunks - 1 - c, n_chunks - 1 - (c - ctx_chunks))
    return jnp.where(d == 0, c, rev)


def _s5_call(cu_t, bmat, cmat, a_r, a_i, ctx_len):
    s_len, n_b, _ = cu_t.shape
    n_chunks = s_len // S5_T
    ctx_chunks = ctx_len // S5_T
    cmap = lambda d, c: (_s5_chunk(d, c, n_chunks, ctx_chunks), 0, 0)
    kern = functools.partial(_s5_kernel, n_b=n_b)
    return pl.pallas_call(
        kern,
        grid=(2, n_chunks),
        in_specs=[pl.BlockSpec((S5_T, n_b, GROUP_WIDTH), cmap),
                  pl.BlockSpec((1, GROUP_WIDTH, 2 * N_STATE), lambda d, c: (d, 0, 0)),
                  pl.BlockSpec((1, 2 * N_STATE, GROUP_WIDTH), lambda d, c: (d, 0, 0)),
                  pl.BlockSpec((1, 1, N_STATE), lambda d, c: (d, 0, 0)),
                  pl.BlockSpec((1, 1, N_STATE), lambda d, c: (d, 0, 0))],
        out_specs=pl.BlockSpec((1, S5_T, n_b, GROUP_WIDTH),
                               lambda d, c: (d, _s5_chunk(d, c, n_chunks, ctx_chunks), 0, 0)),
        out_shape=jax.ShapeDtypeStruct((2, s_len, n_b, GROUP_WIDTH), F32),
        scratch_shapes=[pltpu.VMEM((n_b, N_STATE), F32), pltpu.VMEM((n_b, N_STATE), F32),
                        pltpu.VMEM((S5_T * n_b, 2 * N_STATE), F32)],
        compiler_params=_cparams(("arbitrary", "arbitrary")),
        name="s5_scan",
    )(cu_t, bmat, cmat, a_r, a_i)


def _s5_glu_kernel(u_ref, y_ref, d_ref, w_ref, o_ref):
    y = u_ref[...] * d_ref[...] + y_ref[0] + y_ref[1]
    z = _dot(jax.nn.gelu(y).astype(BF16), w_ref[...])
    o_ref[...] = z[:, :GROUP_WIDTH] * jax.nn.sigmoid(z[:, GROUP_WIDTH:])


def _s5_glu_call(cu_flat, y_dirs, d_skip, w_glu):
    n = cu_flat.shape[0]
    tr = 1024
    return pl.pallas_call(
        _s5_glu_kernel,
        grid=(n // tr,),
        in_specs=[pl.BlockSpec((tr, GROUP_WIDTH), lambda i: (i, 0)),
                  pl.BlockSpec((2, tr, GROUP_WIDTH), lambda i: (0, i, 0)),
                  pl.BlockSpec((1, GROUP_WIDTH), lambda i: (0, 0)),
                  pl.BlockSpec((GROUP_WIDTH, 2 * GROUP_WIDTH), lambda i: (0, 0))],
        out_specs=pl.BlockSpec((tr, GROUP_WIDTH), lambda i: (i, 0)),
        out_shape=jax.ShapeDtypeStruct((n, GROUP_WIDTH), F32),
        compiler_params=_cparams(("arbitrary",)),
        name="s5_glu",
    )(cu_flat, y_dirs, d_skip, w_glu)


NA_KEYS = NA_KH * GRID_W


NA_RB = 4


def _na_kernel(q_ref, k_ref, v_ref, bias_ref, o_ref, *, ctx_len, n_rows):
    j = pl.program_id(1)
    ctx_steps = ctx_len // (NA_RB * GRID_W)
    kc = k_ref[0, 0:ctx_len, :]
    vc = v_ref[0, 0:ctx_len, :]
    scores = []
    for i in range(NA_RB):
        r = (j - ctx_steps) * NA_RB + i
        r0 = jnp.clip(r - NA_KH // 2, 0, n_rows - NA_KH)
        var = jnp.where(j < ctx_steps, NA_KH, r - r0)
        ks = pl.multiple_of(ctx_len + r0 * GRID_W, GRID_W)
        q = q_ref[0, i * GRID_W:(i + 1) * GRID_W, :]
        kl = k_ref[0, pl.ds(ks, NA_KEYS), :]
        for hh in range(N_HEADS):
            sl = slice(hh * HEAD_SLOT, (hh + 1) * HEAD_SLOT)
            scores.append((_nt_dot(q[:, sl], kl[:, sl]) + bias_ref[var, hh], _nt_dot(q[:, sl], kc[:, sl]), ks))
    for i in range(NA_RB):
        qs = slice(i * GRID_W, (i + 1) * GRID_W)
        for hh in range(N_HEADS):
            sl = slice(hh * HEAD_SLOT, (hh + 1) * HEAD_SLOT)
            s_loc, s_ctx, ks = scores[i * N_HEADS + hh]
            mx = jnp.maximum(jnp.max(s_loc, axis=-1, keepdims=True), jnp.max(s_ctx, axis=-1, keepdims=True))
            p_loc = jnp.exp(s_loc - mx)
            p_ctx = jnp.exp(s_ctx - mx)
            den = jnp.sum(p_loc, axis=-1, keepdims=True) + jnp.sum(p_ctx, axis=-1, keepdims=True)
            o = (_dot(p_loc.astype(BF16), v_ref[0, pl.ds(ks, NA_KEYS), sl])
                 + _dot(p_ctx.astype(BF16), vc[:, sl]))
            o_ref[0, qs, sl] = o * (1.0 / den)


def _na_call(qd, kd, vd, bias, ctx_len):
    n_b, s_len, _ = qd.shape
    n_rows = (s_len - ctx_len) // GRID_W
    blk = NA_RB * GRID_W
    assert ctx_len % blk == 0 and n_rows % NA_RB == 0
    kern = functools.partial(_na_kernel, ctx_len=ctx_len, n_rows=n_rows)
    return pl.pallas_call(
        kern,
        grid=(n_b, s_len // blk),
        in_specs=[pl.BlockSpec((1, blk, 4 * HEAD_SLOT), lambda b, j: (b, j, 0)),
                  pl.BlockSpec((1, s_len, 4 * HEAD_SLOT), lambda b, j: (b, 0, 0)),
                  pl.BlockSpec((1, s_len, 4 * HEAD_SLOT), lambda b, j: (b, 0, 0)),
                  pl.BlockSpec(bias.shape, lambda b, j: (0, 0, 0, 0))],
        out_specs=pl.BlockSpec((1, blk, 4 * HEAD_SLOT), lambda b, j: (b, j, 0)),
        out_shape=jax.ShapeDtypeStruct((n_b, s_len, 4 * HEAD_SLOT), F32),
        compiler_params=_cparams(("arbitrary", "arbitrary")),
        name="na_mixer",
    )(qd, kd, vd, bias)


def _out_kernel(ya_ref, yb_ref, ys_ref, yd_ref, x_ref, mod_ref, mg_ref, w_ref, g2_ref, x1_ref, h2_ref):
    m = mod_ref[0, 0]
    mg = mg_ref[...]
    parts = []
    off = 0
    for ref, width in ((ya_ref, 4 * HEAD_SLOT), (yb_ref, 4 * HEAD_SLOT), (ys_ref, GROUP_WIDTH),
                       (yd_ref, 4 * HEAD_SLOT)):
        y_k = ref[...] if ref is ys_ref else ref[0]
        parts.append((_rms(y_k, GROUP_WIDTH) * mg[:, off:off + width]).astype(BF16))
        off += width
    y = _dot(jnp.concatenate(parts, axis=-1), w_ref[...])
    x1 = x_ref[...] + m[2:3] * y
    x1_ref[...] = x1
    h2 = _rms(x1, D_MODEL) * g2_ref[...]
    h2_ref[...] = (h2 * (1.0 + m[4:5]) + m[3:4]).astype(BF16)


def _out_call(ya, yb, ys, yd, x_all, modsel, mixg_p, w_out_p, g2):
    n_b, s_len, _ = ya.shape
    tiles = s_len // TOKEN_TILE
    lat_tiles = tiles - 1
    T = TOKEN_TILE
    xmap = lambda b, t: (_stream_block(b, t, n_b, lat_tiles), 0)
    full = lambda shape: pl.BlockSpec(shape, lambda b, t: (0,) * len(shape))
    ymap = lambda b, t: (b, t, 0)
    return pl.pallas_call(
        _out_kernel,
        grid=(n_b, tiles),
        in_specs=[pl.BlockSpec((1, T, ya.shape[-1]), ymap), pl.BlockSpec((1, T, yb.shape[-1]), ymap),
                  pl.BlockSpec((T, GROUP_WIDTH), lambda b, t: (t, b)),
                  pl.BlockSpec((1, T, yd.shape[-1]), ymap),
                  pl.BlockSpec((T, D_MODEL), xmap),
                  pl.BlockSpec((1, 1, 6, D_MODEL), lambda b, t: (b, jnp.minimum(t, 1), 0, 0)),
                  full(mixg_p.shape), full(w_out_p.shape), full((1, D_MODEL))],
        out_specs=[pl.BlockSpec((T, D_MODEL), xmap), pl.BlockSpec((T, D_MODEL), xmap)],
        out_shape=[jax.ShapeDtypeStruct(x_all.shape, F32), jax.ShapeDtypeStruct(x_all.shape, BF16)],
        compiler_params=_cparams(("arbitrary", "arbitrary")),
        name="out_proj",
    )(ya, yb, ys, yd, x_all, modsel, mixg_p, w_out_p, g2)


PEER_TT = 512
PEER_EB = 1024
PEER_SB = 256
PEER_CW = 256
PEER_IGRP = 2
PEER_PAD = 8


_GELU_B = -2.0 * math.sqrt(2.0 / math.pi) * math.log2(math.e)
_GELU_A = 0.044715 * _GELU_B


def _gelu_tanh(x):
    e = jnp.exp2(x * (x * x * _GELU_A + _GELU_B))
    return x * (1.0 / (1.0 + e))


def _peer_kernel(h2_ref, x1_ref, mod_ref, wq_ref, sk_ref, u_ref, vt_ref, fg_ref, o_ref,
                 h2t_scr, q_scr, g_scr, n_scr, c1_scr, v1_scr, v2_scr, z_scr, p_scr, out_scr,
                 *, n_keys, final):
    e = pl.program_id(1)
    n_e = pl.num_programs(1)
    TT = PEER_TT
    LT = TT // LANES
    NK = n_keys
    e2_row0 = NK + PEER_PAD
    neg = -jnp.inf

    @pl.when(e == 0)
    def _():
        h2 = h2_ref[...]
        h2t_scr[...] = h2.astype(F32).T.astype(BF16)
        q_scr[...] = _dot(h2, wq_ref[...]).astype(BF16)
        out_scr[...] = jnp.zeros_like(out_scr)

        def extract(s, v_scr, ls, want_rank):
            w = s
            rank = jnp.full(s.shape, 127.0, F32) if want_rank else None
            for k in range(PEER_TOPK):
                mk = jnp.max(w, axis=0, keepdims=True)
                v_scr[k:k + 1, ls] = mk
                hit = w == mk
                if want_rank:
                    rank = jnp.where(hit, float(k), rank)
                if k + 1 < PEER_TOPK:
                    w = jnp.where(hit, neg, w)
            return rank

        def head(hh, carry):
            c0 = pl.multiple_of(hh * 256, 256)
            s1_all = _nt_dot(sk_ref[0], q_scr[:, pl.ds(c0, 128)])
            s2_all = _nt_dot(sk_ref[1], q_scr[:, pl.ds(c0 + 128, 128)])
            for lt in range(LT):
                ls = slice(lt * LANES, (lt + 1) * LANES)
                s1 = s1_all[:, ls]
                s2 = s2_all[:, ls]
                extract(s1, v1_scr, ls, False)
                r2 = extract(s2, v2_scr, ls, True)
                slabs = [v1_scr[0:1, ls] + v2_scr[0:16, ls]]
                for a in range(1, 5):
                    slabs.append(v1_scr[a:a + 1, ls] + v2_scr[0:8, ls])
                slabs.append(v1_scr[0:8, ls] + v2_scr[1:2, ls])
                slabs.append(v1_scr[0:8, ls] + v2_scr[0:1, ls])
                slabs.append(v1_scr[8:16, ls] + v2_scr[0:1, ls])
                cand = jnp.concatenate(slabs, axis=0)
                top = v1_scr[0:1, ls] + v2_scr[0:1, ls]
                zsum = jnp.zeros((1, LANES), F32)
                tau = top
                for k in range(PEER_TOPK):
                    tau = jnp.max(cand, axis=0, keepdims=True)
                    zsum = zsum + jnp.exp(tau - top)
                    if k + 1 < PEER_TOPK:
                        cand = jnp.where(cand == tau, neg, cand)
                count = jnp.zeros(s1.shape, F32)
                for a in range(PEER_TOPK):
                    v1a = v1_scr[a:a + 1, ls]
                    sel = (v1a + v2_scr[0:16, ls]) >= tau
                    cnt = jnp.sum(jnp.where(sel, 1.0, 0.0), axis=0, keepdims=True)
                    count = jnp.where(s1 == v1a, cnt, count)
                n_scr[hh, :, ls] = count
                c1_scr[hh, :, ls] = jnp.exp(s1 - v1_scr[0:1, ls]) * (1.0 / zsum)
                g_scr[lt, hh, 0:NK, :] = r2
                g_scr[lt, hh, e2_row0:e2_row0 + NK, :] = jnp.exp(s2 - v2_scr[0:1, ls])
            return carry

        lax.fori_loop(0, PEER_HEADS, head, 0)

    EB = u_ref.shape[0]
    i_per = EB // NK
    i_per_sb = PEER_SB // NK
    jt_n = NK // 16
    lt_per = PEER_CW // LANES
    n_half = TT // PEER_CW
    chunks = [(sb, hf) for hf in range(n_half) for sb in range(EB // PEER_SB)]
    i_grp = min(PEER_IGRP, i_per_sb)

    def stage_z(c):
        sb, hf = chunks[c]
        rows = slice(sb * PEER_SB, (sb + 1) * PEER_SB)
        z = _dot(u_ref[rows, :], h2t_scr[:, hf * PEER_CW:(hf + 1) * PEER_CW])
        for l in range(lt_per):
            z_scr[hf * lt_per + l, rows, :] = z[:, l * LANES:(l + 1) * LANES]

    def stage_gate(c):
        sb, hf = chunks[c]
        for lt in range(hf * lt_per, (hf + 1) * lt_per):
            ls = slice(lt * LANES, (lt + 1) * LANES)
            for g0 in range(0, i_per_sb, i_grp):
                nrow, crow = [], []
                for i2 in range(g0, g0 + i_grp):
                    ii = sb * i_per_sb + i2
                    i8 = pl.multiple_of(e * i_per + (ii // 8) * 8, 8)
                    r8 = slice(ii % 8, ii % 8 + 1)
                    nrow.append([jnp.broadcast_to(n_scr[hh, pl.ds(i8, 8), ls][r8], (8, LANES))
                                 for hh in range(PEER_HEADS)])
                    crow.append([jnp.broadcast_to(c1_scr[hh, pl.ds(i8, 8), ls][r8], (8, LANES))
                                 for hh in range(PEER_HEADS)])
                for jt in range(jt_n):
                    halves = [[None, None] for _ in range(i_grp)]
                    for s8 in range(2):
                        j0 = jt * 16 + s8 * 8
                        w = [None] * i_grp
                        for hh in range(PEER_HEADS):
                            r2t = g_scr[lt, hh, j0:j0 + 8, :]
                            e2t = g_scr[lt, hh, e2_row0 + j0:e2_row0 + j0 + 8, :]
                            for k in range(i_grp):
                                term = jnp.where(r2t < nrow[k][hh], e2t, 0.0) * crow[k][hh]
                                w[k] = term if w[k] is None else w[k] + term
                        for k in range(i_grp):
                            r0 = sb * PEER_SB + (g0 + k) * NK + j0
                            halves[k][s8] = w[k] * _gelu_tanh(z_scr[lt, r0:r0 + 8, :])
                    for k in range(i_grp):
                        r0 = sb * PEER_SB + (g0 + k) * NK + jt * 16
                        p_scr[lt, r0:r0 + 16, :] = jnp.concatenate(halves[k], axis=0).astype(BF16)

    def stage_out(hf):
        p_h = jnp.concatenate([p_scr[hf * lt_per + l] for l in range(lt_per)], axis=1)
        cs = slice(hf * PEER_CW, (hf + 1) * PEER_CW)
        out_scr[:, cs] += _dot(vt_ref[0], p_h)

    for c in range(len(chunks)):
        stage_z(c)
    for c in range(len(chunks)):
        stage_gate(c)
        if (c + 1) % (EB // PEER_SB) == 0:
            stage_out(chunks[c][1])

    @pl.when(e == n_e - 1)
    def _():
        x2 = x1_ref[...] + mod_ref[0, 0][5:6] * out_scr[...].T
        if final:
            x2 = _rms(x2, D_MODEL) * fg_ref[...]
        o_ref[...] = x2


def _peer_call(h2_all, x1_all, modsel, wq, subk, u_tab, v_tab, fg, n_tok, n_lat, l_len, final):
    n_exp = u_tab.shape[0]
    n_keys = subk.shape[1]
    assert n_exp == n_keys * n_keys and n_keys % 16 == 0
    assert min(PEER_EB, n_exp) % (8 * n_keys) == 0
    assert PEER_SB % n_keys == 0 and min(PEER_EB, n_exp) % PEER_SB == 0
    assert n_tok % PEER_TT == 0 and n_lat % PEER_TT == 0 and l_len % PEER_TT == 0
    TT, EB = PEER_TT, min(PEER_EB, n_exp)
    lat_tiles = n_lat // TT
    per_b = l_len // TT

    def mod_map(i, e):
        return (jnp.where(i < lat_tiles, i // per_b, 0), (i < lat_tiles).astype(jnp.int32), 0, 0)

    n_blk = n_exp // EB
    vt_tab = jnp.transpose(v_tab.reshape(n_blk, EB, D_MODEL), (0, 2, 1))
    kern = functools.partial(_peer_kernel, n_keys=n_keys, final=final)
    full = lambda shape: pl.BlockSpec(shape, lambda i, e: (0,) * len(shape))
    return pl.pallas_call(
        kern,
        grid=(n_tok // TT, n_blk),
        in_specs=[pl.BlockSpec((TT, D_MODEL), lambda i, e: (i, 0)),
                  pl.BlockSpec((TT, D_MODEL), lambda i, e: (i, 0)),
                  pl.BlockSpec((1, 1, 6, D_MODEL), mod_map),
                  full(wq.shape), full(subk.shape),
                  pl.BlockSpec((EB, D_MODEL), lambda i, e: (e, 0)),
                  pl.BlockSpec((1, D_MODEL, EB), lambda i, e: (e, 0, 0)),
                  full((1, D_MODEL))],
        out_specs=pl.BlockSpec((TT, D_MODEL), lambda i, e: (i, 0)),
        out_shape=jax.ShapeDtypeStruct((n_tok, D_MODEL), F32),
        scratch_shapes=[pltpu.VMEM((D_MODEL, TT), BF16),
                        pltpu.VMEM((TT, PEER_HEADS * 256), BF16),
                        pltpu.VMEM((TT // LANES, PEER_HEADS, 2 * (n_keys + PEER_PAD), LANES), F32),
                        pltpu.VMEM((PEER_HEADS, n_keys, TT), F32),
                        pltpu.VMEM((PEER_HEADS, n_keys, TT), F32),
                        pltpu.VMEM((PEER_TOPK, TT), F32),
                        pltpu.VMEM((PEER_TOPK, TT), F32),
                        pltpu.VMEM((TT // LANES, EB, LANES), F32),
                        pltpu.VMEM((TT // LANES, EB, LANES), BF16),
                        pltpu.VMEM((D_MODEL, TT), F32)],
        compiler_params=_cparams(("arbitrary", "arbitrary")),
        name="peer_ffn",
    )(h2_all, x1_all, modsel, wq, subk, u_tab, vt_tab, fg)


def _head_slots(w, n_heads, width=HEAD_DIM):
    lead = w.shape[:-1]
    w = w.reshape(lead + (n_heads, width))
    w = jnp.pad(w, [(0, 0)] * len(lead) + [(0, 0), (0, HEAD_SLOT - width)])
    return w.reshape(lead + (n_heads * HEAD_SLOT,))


def _prep_w_in(w_in):
    sizes = (256, 128, 128, B_Q_LORA, B_KV_LORA, B_ROPE, 256, 256, 256, 256)
    offs = np.cumsum((0,) + sizes)
    aq, ak, av, bq, bkv, bkr, cu, dq, dk, dv = [w_in[:, offs[i]:offs[i + 1]] for i in range(10)]
    d = w_in.shape[0]
    bkr_p = jnp.concatenate([jnp.zeros((d, B_NOPE), F32), bkr, jnp.zeros((d, HEAD_SLOT - B_NOPE - B_ROPE), F32)], 1)
    cols = [_head_slots(aq, 4), _head_slots(ak, 2), _head_slots(av, 2),
            jnp.pad(bq, ((0, 0), (0, 256 - B_Q_LORA))), bkv, bkr_p, cu,
            _head_slots(dq, 4), _head_slots(dk, 4), _head_slots(dv, 4)]
    w = jnp.concatenate(cols, axis=1)
    assert w.shape[1] == _C_END
    return w.astype(BF16)


def _prep_mla(w_uq, w_ukv, qg, kvg):
    wq = _head_slots(w_uq, N_HEADS, B_NOPE + B_ROPE)
    wq = jnp.pad(wq, ((0, 256 - B_Q_LORA), (0, 0))).astype(BF16)
    kv = w_ukv.reshape(B_KV_LORA, N_HEADS, B_NOPE + HEAD_DIM)
    wk = _head_slots(kv[:, :, :B_NOPE].reshape(B_KV_LORA, -1), N_HEADS, B_NOPE)
    wv = _head_slots(kv[:, :, B_NOPE:].reshape(B_KV_LORA, -1), N_HEADS, HEAD_DIM)
    wkv = jnp.concatenate([wk, wv], axis=1).astype(BF16)
    qg_p = jnp.pad(qg, (0, 256 - B_Q_LORA)).reshape(1, 256)
    return wq, wkv, qg_p, kvg.reshape(1, B_KV_LORA)


def _rope_tables(ctx_len, l_len):
    t = np.arange(l_len)
    pos = np.stack([t // GRID_W, t % GRID_W], 0).astype(np.float64)

    def build(width, lane0):
        a = width // 2
        half = a // 2
        inv = ROPE_BASE ** (-np.arange(half, dtype=np.float64) / half)
        cos = np.ones((l_len, LANES)); sp = np.zeros((l_len, LANES)); sm = np.zeros((l_len, LANES))
        for j in range(width):
            axis, i = j // a, j % a
            f, second = i % half, i >= half
            ang = pos[axis] * inv[f]
            cos[:, lane0 + j] = np.cos(ang)
            if second:
                sp[:, lane0 + j] = np.sin(ang)
            else:
                sm[:, lane0 + j] = -np.sin(ang)
        tab = np.stack([cos, sp, sm], 0)
        ctx = np.stack([np.ones((ctx_len, LANES)), np.zeros((ctx_len, LANES)), np.zeros((ctx_len, LANES))], 0)
        return jnp.asarray(np.concatenate([ctx, tab], axis=1), F32)

    return build(HEAD_DIM, 0), build(B_ROPE, B_NOPE)


def _s5_matrices(lam_re, lam_im, log_step, b_re, b_im, c_re, c_im):
    lr, li = lam_re.astype(F32), lam_im.astype(F32)
    step = jnp.exp(log_step.astype(F32))[..., None]
    mag = jnp.exp(lr * step)
    abr, abi = mag * jnp.cos(li * step), mag * jnp.sin(li * step)
    nr, ni = abr - 1.0, abi
    den = lr * lr + li * li
    fr = (nr * lr + ni * li) / den
    fi = (ni * lr - nr * li) / den
    bbr = fr[..., None] * b_re - fi[..., None] * b_im
    bbi = fr[..., None] * b_im + fi[..., None] * b_re
    eye = jnp.eye(C_NGROUPS, dtype=F32)
    bm_r = jnp.einsum('dgpc,gh->dgchp', bbr, eye).reshape(2, GROUP_WIDTH, N_STATE)
    bm_i = jnp.einsum('dgpc,gh->dgchp', bbi, eye).reshape(2, GROUP_WIDTH, N_STATE)
    bmat = jnp.concatenate([bm_r, bm_i], axis=2).astype(BF16)
    cm_r = jnp.einsum('dgcp,gh->dgphc', c_re.astype(F32), eye).reshape(2, N_STATE, GROUP_WIDTH)
    cm_i = jnp.einsum('dgcp,gh->dgphc', c_im.astype(F32), eye).reshape(2, N_STATE, GROUP_WIDTH)
    cmat = jnp.concatenate([cm_r, -cm_i], axis=1).astype(BF16)
    return bmat, cmat, abr.reshape(2, 1, N_STATE), abi.reshape(2, 1, N_STATE)


def _na_bias(rpb):
    w = np.arange(GRID_W)
    cs = np.clip(w - NA_KW // 2, 0, GRID_W - NA_KW)
    c = np.arange(GRID_W)
    inwin = (c[None, :] >= cs[:, None]) & (c[None, :] < cs[:, None] + NA_KW)
    colidx = np.clip(c[None, :] - w[:, None] + NA_KW - 1, 0, 2 * NA_KW - 2)
    col_sel = (colidx[:, :, None] == np.arange(2 * NA_KW - 1)).astype(np.float32)
    d, y = np.arange(NA_KH)[:, None], np.arange(NA_KH)[None, :]
    row_sel = ((y - d + NA_KH - 1)[:, :, None] == np.arange(2 * NA_KH - 1)).astype(np.float32)
    hi = lax.Precision.HIGHEST
    t = jnp.einsum('hrq,wcq->hrwc', rpb.astype(F32), jnp.asarray(col_sel), precision=hi)
    b = jnp.einsum('dyr,hrwc->dhwyc', jnp.asarray(row_sel), t, precision=hi)
    b = jnp.where(jnp.asarray(inwin)[None, None, :, None, :], b, NEG_INF)
    b = b.reshape(NA_KH, N_HEADS, GRID_W, NA_KEYS)
    return jnp.concatenate([b, jnp.full((1, N_HEADS, GRID_W, NA_KEYS), NEG_INF, F32)], axis=0)


def _mix_layout(mix_norm_g, w_out):
    g = mix_norm_g.reshape(4, GROUP_WIDTH)
    w = w_out.reshape(4, GROUP_WIDTH, D_MODEL)
    gs, ws = [], []
    for k in range(4):
        if k == 2:
            gs.append(g[k]); ws.append(w[k])
        else:
            gs.append(_head_slots(g[k], N_HEADS))
            wk = w[k].reshape(N_HEADS, HEAD_DIM, D_MODEL)
            wk = jnp.pad(wk, ((0, 0), (0, HEAD_SLOT - HEAD_DIM), (0, 0))).reshape(N_HEADS * HEAD_SLOT, D_MODEL)
            ws.append(wk)
    return jnp.concatenate(gs).reshape(1, -1), jnp.concatenate(ws, axis=0).astype(BF16)


def kernel(x, c, ctx, c_ctx, norm1_g, norm2_g, w_ada, b_ada, w_in, swa_sink, mla_q_norm_g, mla_w_uq,
           mla_kv_norm_g, mla_w_ukv, s5_lambda_re, s5_lambda_im, s5_log_step, s5_b_re, s5_b_im, s5_c_re,
           s5_c_im, s5_d, s5_w_glu, na_rpb, mix_norm_g, w_out, peer_w_q, peer_sub_keys, peer_u, peer_v,
           final_norm_g):
    n_b, l_len, d = x.shape
    ctx_len = ctx.shape[1]
    depth = w_in.shape[0]
    s_len = ctx_len + l_len
    assert d == D_MODEL and ctx_len == TOKEN_TILE and n_b % 8 == 0 and n_b < 16
    assert l_len // GRID_W >= NA_KH and l_len % PEER_TT == 0
    n_lat = n_b * l_len

    cc = jnp.concatenate([c, c_ctx[None], jnp.zeros((16 - n_b - 1, d), F32)], axis=0)
    mod = _ada_call(cc, w_ada, b_ada).reshape(depth, 16, 6, d)
    lat_mod = mod[:, :n_b]
    ctx_mod = jnp.broadcast_to(mod[:, n_b:n_b + 1], lat_mod.shape)
    modsel = jnp.stack([ctx_mod, lat_mod], axis=2)

    tab_a, tab_b = _rope_tables(ctx_len, l_len)
    x_all = jnp.concatenate([x.reshape(n_lat, d), ctx.reshape(n_b * ctx_len, d)], axis=0)
    fg = final_norm_g.reshape(1, d)

    for l in range(depth):
        last = l == depth - 1
        w_in_p = _prep_w_in(w_in[l])
        wuq, wukv, qg, kvg = _prep_mla(mla_w_uq[l], mla_w_ukv[l], mla_q_norm_g[l], mla_kv_norm_g[l])
        qa, ka, va, qb, kb, vb, cu, qd, kd, vd = _proj_call(
            x_all, modsel[l], norm1_g[l].reshape(1, d), w_in_p, tab_a, tab_b, qg, wuq, kvg, wukv, n_b, s_len)

        ya = _swa_call(swa_sink[l], qa, ka, va, ctx_len)
        yb = _mla_call(qb, kb, vb, ctx_len)

        bmat, cmat, a_r, a_i = _s5_matrices(s5_lambda_re[l], s5_lambda_im[l], s5_log_step[l],
                                            s5_b_re[l], s5_b_im[l], s5_c_re[l], s5_c_im[l])
        cu_t = cu.reshape(s_len, n_b, GROUP_WIDTH)
        y_dirs = _s5_call(cu_t, bmat, cmat, a_r, a_i, ctx_len)
        ys_t = _s5_glu_call(cu.reshape(s_len * n_b, GROUP_WIDTH), y_dirs.reshape(2, s_len * n_b, GROUP_WIDTH),
                            s5_d[l].reshape(1, GROUP_WIDTH), s5_w_glu[l].astype(BF16))
        ys = ys_t.reshape(s_len, n_b * GROUP_WIDTH)

        yd = _na_call(qd, kd, vd, _na_bias(na_rpb[l]), ctx_len)

        mixg_p, w_out_p = _mix_layout(mix_norm_g[l], w_out[l])
        x1_all, h2_all = _out_call(ya, yb, ys, yd, x_all, modsel[l], mixg_p, w_out_p, norm2_g[l].reshape(1, d))

        n_tok = n_lat if last else x_all.shape[0]
        x_all = _peer_call(h2_all, x1_all, modsel[l], peer_w_q[l].astype(BF16), peer_sub_keys[l].astype(BF16),
                           peer_u[l].astype(BF16), peer_v[l].astype(BF16), fg, n_tok, n_lat, l_len, last)

    return x_all[:n_lat].reshape(n_b, l_len, d)
```

```python
import functools
import math

import numpy as np
import jax
import jax.numpy as jnp
from jax import lax
from jax.experimental import pallas as pl
from jax.experimental.pallas import tpu as pltpu

F32 = jnp.float32
BF16 = jnp.bfloat16

D_MODEL = 1024
HEAD_DIM = 64
GROUP_WIDTH = 256
GRID_W = 64
EPS = 1e-6
NEG_INF = -1e30
ROPE_BASE = 10000.0
N_HEADS = 4
A_KV_HEADS = 2
A_WINDOW = 128
A_BLOCK = 128
B_NOPE = 64
B_ROPE = 32
B_Q_LORA = 192
B_KV_LORA = 128
C_GROUP = 16
C_NGROUPS = 16
C_STATE = 64
NA_KH = 8
NA_KW = 16
PEER_HEADS = 8
PEER_TOPK = 16

LANES = 128
HEAD_SLOT = LANES
TOKEN_TILE = 256
VMEM_LIMIT = 60 * 1024 * 1024

N_STATE = C_NGROUPS * C_STATE


def _cparams(sem):
    return pltpu.CompilerParams(dimension_semantics=sem, vmem_limit_bytes=VMEM_LIMIT)


def _nt_dot(a, b):
    return lax.dot_general(a, b, (((1,), (1,)), ((), ())), preferred_element_type=F32)


def _dot(a, b):
    return jnp.dot(a, b, preferred_element_type=F32)


def _ada_kernel(c_ref, w_ref, b_ref, o_ref):
    c = c_ref[...]
    s = c * jax.nn.sigmoid(c)
    o_ref[0] = _dot(s.astype(BF16), w_ref[0].astype(BF16)) + b_ref[0]


def _ada_call(cc, w_ada, b_ada):
    depth, d, n6 = w_ada.shape
    tn = 1536
    return pl.pallas_call(
        _ada_kernel,
        grid=(depth, n6 // tn),
        in_specs=[pl.BlockSpec((cc.shape[0], d), lambda l, j: (0, 0)),
                  pl.BlockSpec((1, d, tn), lambda l, j: (l, 0, j)),
                  pl.BlockSpec((1, 1, tn), lambda l, j: (l, 0, j))],
        out_specs=pl.BlockSpec((1, cc.shape[0], tn), lambda l, j: (l, 0, j)),
        out_shape=jax.ShapeDtypeStruct((depth, cc.shape[0], n6), F32),
        compiler_params=_cparams(("arbitrary", "arbitrary")),
        name="ada_mod",
    )(cc, w_ada, b_ada.reshape(depth, 1, n6))


_C_AQ = 0
_C_AK = _C_AQ + 4 * HEAD_SLOT
_C_AV = _C_AK + 2 * HEAD_SLOT
_C_BQ = _C_AV + 2 * HEAD_SLOT
_C_BKV = _C_BQ + 256
_C_BKR = _C_BKV + 128
_C_CU = _C_BKR + 128
_C_DQ = _C_CU + 256
_C_DK = _C_DQ + 4 * HEAD_SLOT
_C_DV = _C_DK + 4 * HEAD_SLOT
_C_END = _C_DV + 4 * HEAD_SLOT


def _rms(x, n):
    return x * lax.rsqrt(jnp.sum(x * x, axis=-1, keepdims=True) * (1.0 / n) + EPS)


def _rope(t, tab_ref, shift):
    return (t * tab_ref[0]
            + pltpu.roll(t, shift, 1) * tab_ref[1]
            + pltpu.roll(t, LANES - shift, 1) * tab_ref[2])


def _proj_kernel(x_ref, mod_ref, g_ref, w_ref, ta_ref, tb_ref, qg_ref, wuq_ref, kvg_ref, wukv_ref,
                 qa_ref, ka_ref, va_ref, qb_ref, kb_ref, vb_ref, cu_ref, qd_ref, kd_ref, vd_ref):
    x = x_ref[...]
    m = mod_ref[0, 0]
    h = _rms(x, D_MODEL) * g_ref[...]
    h = h * (1.0 + m[1:2]) + m[0:1]
    z = _dot(h.astype(BF16), w_ref[...])

    a_scale = HEAD_DIM ** -0.5
    for hh in range(N_HEADS):
        c0 = _C_AQ + hh * HEAD_SLOT
        qa_ref[0, :, hh * HEAD_SLOT:(hh + 1) * HEAD_SLOT] = (
            _rope(z[:, c0:c0 + HEAD_SLOT], ta_ref, 16) * a_scale).astype(BF16)
    for g in range(A_KV_HEADS):
        c0 = _C_AK + g * HEAD_SLOT
        ka_ref[0, :, g * HEAD_SLOT:(g + 1) * HEAD_SLOT] = _rope(z[:, c0:c0 + HEAD_SLOT], ta_ref, 16).astype(BF16)
    va_ref[0] = z[:, _C_AV:_C_AV + 2 * HEAD_SLOT].astype(BF16)

    cq = _rms(z[:, _C_BQ:_C_BQ + 256], B_Q_LORA) * qg_ref[...]
    qb = _dot(cq.astype(BF16), wuq_ref[...])
    ckv = _rms(z[:, _C_BKV:_C_BKV + 128], B_KV_LORA) * kvg_ref[...]
    kv = _dot(ckv.astype(BF16), wukv_ref[...])
    kr = _rope(z[:, _C_BKR:_C_BKR + 128], tb_ref, 8)
    lane = lax.broadcasted_iota(jnp.int32, (1, HEAD_SLOT), 1)
    ones_col = (lane == HEAD_DIM).astype(F32)
    for hh in range(N_HEADS):
        sl = slice(hh * HEAD_SLOT, (hh + 1) * HEAD_SLOT)
        qb_ref[0, :, sl] = _rope(qb[:, sl], tb_ref, 8).astype(BF16)
        kb_ref[0, :, sl] = (kv[:, sl] + kr).astype(BF16)
        vsl = slice(4 * HEAD_SLOT + hh * HEAD_SLOT, 4 * HEAD_SLOT + (hh + 1) * HEAD_SLOT)
        vb_ref[0, :, sl] = (kv[:, vsl] + ones_col).astype(BF16)

    cu_ref[...] = z[:, _C_CU:_C_CU + 256]
    qd_ref[0] = (z[:, _C_DQ:_C_DQ + 4 * HEAD_SLOT] * a_scale).astype(BF16)
    kd_ref[0] = z[:, _C_DK:_C_DK + 4 * HEAD_SLOT].astype(BF16)
    vd_ref[0] = z[:, _C_DV:_C_DV + 4 * HEAD_SLOT].astype(BF16)


def _stream_block(b, t, n_b, lat_tiles):
    return jnp.where(t == 0, n_b * lat_tiles + b, b * lat_tiles + t - 1)


def _proj_call(x_all, modsel, g1, w_in_p, tab_a, tab_b, qg, wuq, kvg, wukv, n_b, s_len):
    tiles = s_len // TOKEN_TILE
    lat_tiles = tiles - 1
    T = TOKEN_TILE
    xmap = lambda b, t: (_stream_block(b, t, n_b, lat_tiles), 0)
    full = lambda shape: pl.BlockSpec(shape, lambda b, t: (0,) * len(shape))
    omap = lambda b, t: (b, t, 0)
    widths = [4 * HEAD_SLOT, 2 * HEAD_SLOT, 2 * HEAD_SLOT, 4 * HEAD_SLOT, 4 * HEAD_SLOT, 4 * HEAD_SLOT,
              256, 4 * HEAD_SLOT, 4 * HEAD_SLOT, 4 * HEAD_SLOT]
    dts = [BF16, BF16, BF16, BF16, BF16, BF16, F32, BF16, BF16, BF16]
    return pl.pallas_call(
        _proj_kernel,
        grid=(n_b, tiles),
        in_specs=[pl.BlockSpec((T, D_MODEL), xmap),
                  pl.BlockSpec((1, 1, 6, D_MODEL), lambda b, t: (b, jnp.minimum(t, 1), 0, 0)),
                  full((1, D_MODEL)),
                  full(w_in_p.shape),
                  pl.BlockSpec((3, T, LANES), lambda b, t: (0, t, 0)),
                  pl.BlockSpec((3, T, LANES), lambda b, t: (0, t, 0)),
                  full(qg.shape), full(wuq.shape), full(kvg.shape), full(wukv.shape)],
        out_specs=[pl.BlockSpec((T, w), lambda b, t: (t, b)) if i == 6 else pl.BlockSpec((1, T, w), omap)
                   for i, w in enumerate(widths)],
        out_shape=[jax.ShapeDtypeStruct((s_len, n_b * w) if i == 6 else (n_b, s_len, w), dt)
                   for i, (w, dt) in enumerate(zip(widths, dts))],
        compiler_params=_cparams(("arbitrary", "arbitrary")),
        name="mod_proj",
    )(x_all, modsel, g1, w_in_p, tab_a, tab_b, qg, wuq, kvg, wukv)


def _swa_kernel(sink_ref, q_ref, k_ref, v_ref, o_ref, *, ctx_len, s_len):
    n = pl.program_id(1)
    is_lat = n >= ctx_len // A_BLOCK
    band = 3 * A_BLOCK
    ks = pl.multiple_of(jnp.clip((n - 1) * A_BLOCK, 0, s_len - band), A_BLOCK)
    q = q_ref[0]
    kl = k_ref[0, pl.ds(ks, band), :]
    vl = v_ref[0, pl.ds(ks, band), :]
    kc = k_ref[0, 0:ctx_len, :]
    vc = v_ref[0, 0:ctx_len, :]
    qpos = n * A_BLOCK + lax.broadcasted_iota(jnp.int32, (A_BLOCK, band), 0)
    kpos = ks + lax.broadcasted_iota(jnp.int32, (A_BLOCK, band), 1)
    window = jnp.where(is_lat, A_WINDOW, -1)
    valid = (jnp.abs(qpos - kpos) <= window) & (kpos >= ctx_len)
    rep = N_HEADS // A_KV_HEADS
    scores = []
    for hh in range(N_HEADS):
        qs = slice(hh * HEAD_SLOT, (hh + 1) * HEAD_SLOT)
        gs = slice((hh // rep) * HEAD_SLOT, (hh // rep + 1) * HEAD_SLOT)
        scores.append((_nt_dot(q[:, qs], kl[:, gs]), _nt_dot(q[:, qs], kc[:, gs])))
    for hh in range(N_HEADS):
        g = hh // rep
        qs = slice(hh * HEAD_SLOT, (hh + 1) * HEAD_SLOT)
        gs = slice(g * HEAD_SLOT, (g + 1) * HEAD_SLOT)
        s_loc = jnp.where(valid, scores[hh][0], NEG_INF)
        s_ctx = scores[hh][1]
        sk = sink_ref[hh]
        mx = jnp.maximum(jnp.max(s_loc, axis=-1, keepdims=True), jnp.max(s_ctx, axis=-1, keepdims=True))
        mx = jnp.maximum(mx, sk)
        p_loc = jnp.exp(s_loc - mx)
        p_ctx = jnp.exp(s_ctx - mx)
        den = (jnp.sum(p_loc, axis=-1, keepdims=True) + jnp.sum(p_ctx, axis=-1, keepdims=True)
               + jnp.exp(sk - mx))
        o = _dot(p_loc.astype(BF16), vl[:, gs]) + _dot(p_ctx.astype(BF16), vc[:, gs])
        o_ref[0, :, qs] = o * (1.0 / den)


def _swa_call(sink, qa, ka, va, ctx_len):
    n_b, s_len, _ = qa.shape
    kern = functools.partial(_swa_kernel, ctx_len=ctx_len, s_len=s_len)
    return pl.pallas_call(
        kern,
        grid=(n_b, s_len // A_BLOCK),
        in_specs=[pl.BlockSpec(memory_space=pltpu.SMEM),
                  pl.BlockSpec((1, A_BLOCK, 4 * HEAD_SLOT), lambda b, n: (b, n, 0)),
                  pl.BlockSpec((1, s_len, 2 * HEAD_SLOT), lambda b, n: (b, 0, 0)),
                  pl.BlockSpec((1, s_len, 2 * HEAD_SLOT), lambda b, n: (b, 0, 0))],
        out_specs=pl.BlockSpec((1, A_BLOCK, 4 * HEAD_SLOT), lambda b, n: (b, n, 0)),
        out_shape=jax.ShapeDtypeStruct((n_b, s_len, 4 * HEAD_SLOT), F32),
        compiler_params=_cparams(("arbitrary", "arbitrary")),
        name="swa_mixer",
    )(sink, qa, ka, va)


MLA_TQ = 256


def _mla_kernel(q_ref, k_ref, v_ref, o_ref, *, ctx_len, s_len):
    t = pl.program_id(1)
    is_lat = t >= ctx_len // MLA_TQ
    scale = (B_NOPE + B_ROPE) ** -0.5
    lane = lax.broadcasted_iota(jnp.int32, (1, HEAD_SLOT), 1)

    def attend(n_keys):
        def scores(hh):
            sl = slice(hh * HEAD_SLOT, (hh + 1) * HEAD_SLOT)
            return _nt_dot(q_ref[0, :, sl], k_ref[0, 0:n_keys, sl]) * scale

        s_next = scores(0)
        for hh in range(N_HEADS):
            sl = slice(hh * HEAD_SLOT, (hh + 1) * HEAD_SLOT)
            s = s_next
            if hh + 1 < N_HEADS:
                s_next = scores(hh + 1)
            p = jnp.exp(s - jnp.max(s, axis=-1, keepdims=True))
            acc = _dot(p.astype(BF16), v_ref[0, 0:n_keys, sl])
            den = acc[:, HEAD_DIM:HEAD_DIM + 1]
            o_ref[0, :, sl] = jnp.where(lane < HEAD_DIM, acc * (1.0 / den), 0.0)

    @pl.when(is_lat)
    def _():
        attend(s_len)

    @pl.when(jnp.logical_not(is_lat))
    def _():
        attend(ctx_len)


def _mla_call(qb, kb, vb, ctx_len):
    n_b, s_len, _ = qb.shape
    assert ctx_len % MLA_TQ == 0 and s_len % MLA_TQ == 0
    kern = functools.partial(_mla_kernel, ctx_len=ctx_len, s_len=s_len)
    return pl.pallas_call(
        kern,
        grid=(n_b, s_len // MLA_TQ),
        in_specs=[pl.BlockSpec((1, MLA_TQ, 4 * HEAD_SLOT), lambda b, t: (b, t, 0)),
                  pl.BlockSpec((1, s_len, 4 * HEAD_SLOT), lambda b, t: (b, 0, 0)),
                  pl.BlockSpec((1, s_len, 4 * HEAD_SLOT), lambda b, t: (b, 0, 0))],
        out_specs=pl.BlockSpec((1, MLA_TQ, 4 * HEAD_SLOT), lambda b, t: (b, t, 0)),
        out_shape=jax.ShapeDtypeStruct((n_b, s_len, 4 * HEAD_SLOT), F32),
        compiler_params=_cparams(("arbitrary", "arbitrary")),
        name="mla_mixer",
    )(qb, kb, vb)


S5_T = 128


def _s5_kernel(u_ref, bm_ref, cm_ref, ar_ref, ai_ref, y_ref, hr_scr, hi_scr, bu_scr, *, n_b):
    d = pl.program_id(0)
    c = pl.program_id(1)

    @pl.when(c == 0)
    def _():
        hr_scr[...] = jnp.zeros_like(hr_scr)
        hi_scr[...] = jnp.zeros_like(hi_scr)

    u = u_ref[...].reshape(S5_T * n_b, GROUP_WIDTH)
    bu_scr[...] = _dot(u.astype(BF16), bm_ref[0])
    a_r = jnp.broadcast_to(ar_ref[0], (n_b, N_STATE))
    a_i = jnp.broadcast_to(ai_ref[0], (n_b, N_STATE))

    def step(i, carry):
        h_r, h_i = carry
        t = jnp.where(d == 0, i, S5_T - 1 - i)
        row = pl.multiple_of(t * n_b, n_b)
        b_r = bu_scr[pl.ds(row, n_b), 0:N_STATE]
        b_i = bu_scr[pl.ds(row, n_b), N_STATE:2 * N_STATE]
        n_r = a_r * h_r - a_i * h_i + b_r
        n_i = a_r * h_i + a_i * h_r + b_i
        bu_scr[pl.ds(row, n_b), 0:N_STATE] = n_r
        bu_scr[pl.ds(row, n_b), N_STATE:2 * N_STATE] = n_i
        return n_r, n_i

    h_r, h_i = lax.fori_loop(0, S5_T, step, (hr_scr[...], hi_scr[...]), unroll=4)
    hr_scr[...] = h_r
    hi_scr[...] = h_i
    y = _dot(bu_scr[...].astype(BF16), cm_ref[0])
    y_ref[0] = y.reshape(S5_T, n_b, GROUP_WIDTH)


def _s5_chunk(d, c, n_chunks, ctx_chunks):
    rev = jnp.where(c < ctx_chunks, ctx_chunks - 1 - c, n_chunks - 1 - (c - ctx_chunks))
    return jnp.where(d == 0, c, rev)


def _s5_call(cu_t, bmat, cmat, a_r, a_i, ctx_len):
    s_len, n_b, _ = cu_t.shape
    n_chunks = s_len // S5_T
    ctx_chunks = ctx_len // S5_T
    cmap = lambda d, c: (_s5_chunk(d, c, n_chunks, ctx_chunks), 0, 0)
    kern = functools.partial(_s5_kernel, n_b=n_b)
    return pl.pallas_call(
        kern,
        grid=(2, n_chunks),
        in_specs=[pl.BlockSpec((S5_T, n_b, GROUP_WIDTH), cmap),
                  pl.BlockSpec((1, GROUP_WIDTH, 2 * N_STATE), lambda d, c: (d, 0, 0)),
                  pl.BlockSpec((1, 2 * N_STATE, GROUP_WIDTH), lambda d, c: (d, 0, 0)),
                  pl.BlockSpec((1, 1, N_STATE), lambda d, c: (d, 0, 0)),
                  pl.BlockSpec((1, 1, N_STATE), lambda d, c: (d, 0, 0))],
        out_specs=pl.BlockSpec((1, S5_T, n_b, GROUP_WIDTH),
                               lambda d, c: (d, _s5_chunk(d, c, n_chunks, ctx_chunks), 0, 0)),
        out_shape=jax.ShapeDtypeStruct((2, s_len, n_b, GROUP_WIDTH), F32),
        scratch_shapes=[pltpu.VMEM((n_b, N_STATE), F32), pltpu.VMEM((n_b, N_STATE), F32),
                        pltpu.VMEM((S5_T * n_b, 2 * N_STATE), F32)],
        compiler_params=_cparams(("arbitrary", "arbitrary")),
        name="s5_scan",
    )(cu_t, bmat, cmat, a_r, a_i)


def _s5_glu_kernel(u_ref, y_ref, d_ref, w_ref, o_ref):
    y = u_ref[...] * d_ref[...] + y_ref[0] + y_ref[1]
    z = _dot(jax.nn.gelu(y).astype(BF16), w_ref[...])
    o_ref[...] = z[:, :GROUP_WIDTH] * jax.nn.sigmoid(z[:, GROUP_WIDTH:])


def _s5_glu_call(cu_flat, y_dirs, d_skip, w_glu):
    n = cu_flat.shape[0]
    tr = 1024
    return pl.pallas_call(
        _s5_glu_kernel,
        grid=(n // tr,),
        in_specs=[pl.BlockSpec((tr, GROUP_WIDTH), lambda i: (i, 0)),
                  pl.BlockSpec((2, tr, GROUP_WIDTH), lambda i: (0, i, 0)),
                  pl.BlockSpec((1, GROUP_WIDTH), lambda i: (0, 0)),
                  pl.BlockSpec((GROUP_WIDTH, 2 * GROUP_WIDTH), lambda i: (0, 0))],
        out_specs=pl.BlockSpec((tr, GROUP_WIDTH), lambda i: (i, 0)),
        out_shape=jax.ShapeDtypeStruct((n, GROUP_WIDTH), F32),
        compiler_params=_cparams(("arbitrary",)),
        name="s5_glu",
    )(cu_flat, y_dirs, d_skip, w_glu)


NA_KEYS = NA_KH * GRID_W


NA_RB = 4


def _na_kernel(q_ref, k_ref, v_ref, bias_ref, o_ref, *, ctx_len, n_rows):
    j = pl.program_id(1)
    ctx_steps = ctx_len // (NA_RB * GRID_W)
    kc = k_ref[0, 0:ctx_len, :]
    vc = v_ref[0, 0:ctx_len, :]
    scores = []
    for i in range(NA_RB):
        r = (j - ctx_steps) * NA_RB + i
        r0 = jnp.clip(r - NA_KH // 2, 0, n_rows - NA_KH)
        var = jnp.where(j < ctx_steps, NA_KH, r - r0)
        ks = pl.multiple_of(ctx_len + r0 * GRID_W, GRID_W)
        q = q_ref[0, i * GRID_W:(i + 1) * GRID_W, :]
        kl = k_ref[0, pl.ds(ks, NA_KEYS), :]
        for hh in range(N_HEADS):
            sl = slice(hh * HEAD_SLOT, (hh + 1) * HEAD_SLOT)
            scores.append((_nt_dot(q[:, sl], kl[:, sl]) + bias_ref[var, hh], _nt_dot(q[:, sl], kc[:, sl]), ks))
    for i in range(NA_RB):
        qs = slice(i * GRID_W, (i + 1) * GRID_W)
        for hh in range(N_HEADS):
            sl = slice(hh * HEAD_SLOT, (hh + 1) * HEAD_SLOT)
            s_loc, s_ctx, ks = scores[i * N_HEADS + hh]
            mx = jnp.maximum(jnp.max(s_loc, axis=-1, keepdims=True), jnp.max(s_ctx, axis=-1, keepdims=True))
            p_loc = jnp.exp(s_loc - mx)
            p_ctx = jnp.exp(s_ctx - mx)
            den = jnp.sum(p_loc, axis=-1, keepdims=True) + jnp.sum(p_ctx, axis=-1, keepdims=True)
            o = (_dot(p_loc.astype(BF16), v_ref[0, pl.ds(ks, NA_KEYS), sl])
                 + _dot(p_ctx.astype(BF16), vc[:, sl]))
            o_ref[0, qs, sl] = o * (1.0 / den)


def _na_call(qd, kd, vd, bias, ctx_len):
    n_b, s_len, _ = qd.shape
    n_rows = (s_len - ctx_len) // GRID_W
    blk = NA_RB * GRID_W
    assert ctx_len % blk == 0 and n_rows % NA_RB == 0
    kern = functools.partial(_na_kernel, ctx_len=ctx_len, n_rows=n_rows)
    return pl.pallas_call(
        kern,
        grid=(n_b, s_len // blk),
        in_specs=[pl.BlockSpec((1, blk, 4 * HEAD_SLOT), lambda b, j: (b, j, 0)),
                  pl.BlockSpec((1, s_len, 4 * HEAD_SLOT), lambda b, j: (b, 0, 0)),
                  pl.BlockSpec((1, s_len, 4 * HEAD_SLOT), lambda b, j: (b, 0, 0)),
                  pl.BlockSpec(bias.shape, lambda b, j: (0, 0, 0, 0))],
        out_specs=pl.BlockSpec((1, blk, 4 * HEAD_SLOT), lambda b, j: (b, j, 0)),
        out_shape=jax.ShapeDtypeStruct((n_b, s_len, 4 * HEAD_SLOT), F32),
        compiler_params=_cparams(("arbitrary", "arbitrary")),
        name="na_mixer",
    )(qd, kd, vd, bias)


def _out_kernel(ya_ref, yb_ref, ys_ref, yd_ref, x_ref, mod_ref, mg_ref, w_ref, g2_ref, x1_ref, h2_ref):
    m = mod_ref[0, 0]
    mg = mg_ref[...]
    parts = []
    off = 0
    for ref, width in ((ya_ref, 4 * HEAD_SLOT), (yb_ref, 4 * HEAD_SLOT), (ys_ref, GROUP_WIDTH),
                       (yd_ref, 4 * HEAD_SLOT)):
        y_k = ref[...] if ref is ys_ref else ref[0]
        parts.append((_rms(y_k, GROUP_WIDTH) * mg[:, off:off + width]).astype(BF16))
        off += width
    y = _dot(jnp.concatenate(parts, axis=-1), w_ref[...])
    x1 = x_ref[...] + m[2:3] * y
    x1_ref[...] = x1
    h2 = _rms(x1, D_MODEL) * g2_ref[...]
    h2_ref[...] = (h2 * (1.0 + m[4:5]) + m[3:4]).astype(BF16)


def _out_call(ya, yb, ys, yd, x_all, modsel, mixg_p, w_out_p, g2):
    n_b, s_len, _ = ya.shape
    tiles = s_len // TOKEN_TILE
    lat_tiles = tiles - 1
    T = TOKEN_TILE
    xmap = lambda b, t: (_stream_block(b, t, n_b, lat_tiles), 0)
    full = lambda shape: pl.BlockSpec(shape, lambda b, t: (0,) * len(shape))
    ymap = lambda b, t: (b, t, 0)
    return pl.pallas_call(
        _out_kernel,
        grid=(n_b, tiles),
        in_specs=[pl.BlockSpec((1, T, ya.shape[-1]), ymap), pl.BlockSpec((1, T, yb.shape[-1]), ymap),
                  pl.BlockSpec((T, GROUP_WIDTH), lambda b, t: (t, b)),
                  pl.BlockSpec((1, T, yd.shape[-1]), ymap),
                  pl.BlockSpec((T, D_MODEL), xmap),
                  pl.BlockSpec((1, 1, 6, D_MODEL), lambda b, t: (b, jnp.minimum(t, 1), 0, 0)),
                  full(mixg_p.shape), full(w_out_p.shape), full((1, D_MODEL))],
        out_specs=[pl.BlockSpec((T, D_MODEL), xmap), pl.BlockSpec((T, D_MODEL), xmap)],
        out_shape=[jax.ShapeDtypeStruct(x_all.shape, F32), jax.ShapeDtypeStruct(x_all.shape, BF16)],
        compiler_params=_cparams(("arbitrary", "arbitrary")),
        name="out_proj",
    )(ya, yb, ys, yd, x_all, modsel, mixg_p, w_out_p, g2)


PEER_TT = 512
PEER_EB = 2048
PEER_SB = 256
PEER_CW = 256
PEER_IGRP = 2
PEER_PAD = 8


_GELU_B = -2.0 * math.sqrt(2.0 / math.pi) * math.log2(math.e)
_GELU_A = 0.044715 * _GELU_B


def _gelu_tanh(x):
    e = jnp.exp2(x * (x * x * _GELU_A + _GELU_B))
    return x * (1.0 / (1.0 + e))


def _peer_kernel(h2_ref, x1_ref, mod_ref, wq_ref, sk_ref, u_ref, vt_ref, fg_ref, o_ref,
                 h2t_scr, q_scr, g_scr, n_scr, c1_scr, v1_scr, v2_scr, z_scr, p_scr, out_scr,
                 *, n_keys, final):
    e = pl.program_id(1)
    n_e = pl.num_programs(1)
    TT = PEER_TT
    LT = TT // LANES
    NK = n_keys
    e2_row0 = NK + PEER_PAD
    neg = -jnp.inf

    @pl.when(e == 0)
    def _():
        h2 = h2_ref[...]
        h2t_scr[...] = h2.astype(F32).T.astype(BF16)
        q_scr[...] = _dot(h2, wq_ref[...]).astype(BF16)
        out_scr[...] = jnp.zeros_like(out_scr)

        def extract(s, v_scr, ls, want_rank):
            w = s
            rank = jnp.full(s.shape, 127.0, F32) if want_rank else None
            for k in range(PEER_TOPK):
                mk = jnp.max(w, axis=0, keepdims=True)
                v_scr[k:k + 1, ls] = mk
                hit = w == mk
                if want_rank:
                    rank = jnp.where(hit, float(k), rank)
                if k + 1 < PEER_TOPK:
                    w = jnp.where(hit, neg, w)
            return rank

        def head(hh, carry):
            c0 = pl.multiple_of(hh * 256, 256)
            s1_all = _nt_dot(sk_ref[0], q_scr[:, pl.ds(c0, 128)])
            s2_all = _nt_dot(sk_ref[1], q_scr[:, pl.ds(c0 + 128, 128)])
            for lt in range(LT):
                ls = slice(lt * LANES, (lt + 1) * LANES)
                s1 = s1_all[:, ls]
                s2 = s2_all[:, ls]
                extract(s1, v1_scr, ls, False)
                r2 = extract(s2, v2_scr, ls, True)
                slabs = [v1_scr[0:1, ls] + v2_scr[0:16, ls]]
                for a in range(1, 5):
                    slabs.append(v1_scr[a:a + 1, ls] + v2_scr[0:8, ls])
                slabs.append(v1_scr[0:8, ls] + v2_scr[1:2, ls])
                slabs.append(v1_scr[0:8, ls] + v2_scr[0:1, ls])
                slabs.append(v1_scr[8:16, ls] + v2_scr[0:1, ls])
                cand = jnp.concatenate(slabs, axis=0)
                top = v1_scr[0:1, ls] + v2_scr[0:1, ls]
                zsum = jnp.zeros((1, LANES), F32)
                tau = top
                for k in range(PEER_TOPK):
                    tau = jnp.max(cand, axis=0, keepdims=True)
                    zsum = zsum + jnp.exp(tau - top)
                    if k + 1 < PEER_TOPK:
                        cand = jnp.where(cand == tau, neg, cand)
                count = jnp.zeros(s1.shape, F32)
                for a in range(PEER_TOPK):
                    v1a = v1_scr[a:a + 1, ls]
                    sel = (v1a + v2_scr[0:16, ls]) >= tau
                    cnt = jnp.sum(jnp.where(sel, 1.0, 0.0), axis=0, keepdims=True)
                    count = jnp.where(s1 == v1a, cnt, count)
                n_scr[hh, :, ls] = count
                c1_scr[hh, :, ls] = jnp.exp(s1 - v1_scr[0:1, ls]) * (1.0 / zsum)
                g_scr[lt, hh, 0:NK, :] = r2
                g_scr[lt, hh, e2_row0:e2_row0 + NK, :] = jnp.exp(s2 - v2_scr[0:1, ls])
            return carry

        lax.fori_loop(0, PEER_HEADS, head, 0)

    EB = u_ref.shape[0]
    i_per = EB // NK
    i_per_sb = PEER_SB // NK
    jt_n = NK // 16
    lt_per = PEER_CW // LANES
    n_half = TT // PEER_CW
    chunks = [(sb, hf) for hf in range(n_half) for sb in range(EB // PEER_SB)]
    i_grp = min(PEER_IGRP, i_per_sb)

    def stage_z(c):
        sb, hf = chunks[c]
        rows = slice(sb * PEER_SB, (sb + 1) * PEER_SB)
        z = _dot(u_ref[rows, :], h2t_scr[:, hf * PEER_CW:(hf + 1) * PEER_CW])
        for l in range(lt_per):
            z_scr[hf * lt_per + l, rows, :] = z[:, l * LANES:(l + 1) * LANES]

    def stage_gate(c):
        sb, hf = chunks[c]
        for lt in range(hf * lt_per, (hf + 1) * lt_per):
            ls = slice(lt * LANES, (lt + 1) * LANES)
            for g0 in range(0, i_per_sb, i_grp):
                nrow, crow = [], []
                for i2 in range(g0, g0 + i_grp):
                    ii = sb * i_per_sb + i2
                    i8 = pl.multiple_of(e * i_per + (ii // 8) * 8, 8)
                    r8 = slice(ii % 8, ii % 8 + 1)
                    nrow.append([jnp.broadcast_to(n_scr[hh, pl.ds(i8, 8), ls][r8], (8, LANES))
                                 for hh in range(PEER_HEADS)])
                    crow.append([jnp.broadcast_to(c1_scr[hh, pl.ds(i8, 8), ls][r8], (8, LANES))
                                 for hh in range(PEER_HEADS)])
                for jt in range(jt_n):
                    halves = [[None, None] for _ in range(i_grp)]
                    for s8 in range(2):
                        j0 = jt * 16 + s8 * 8
                        w = [None] * i_grp
                        for hh in range(PEER_HEADS):
                            r2t = g_scr[lt, hh, j0:j0 + 8, :]
                            e2t = g_scr[lt, hh, e2_row0 + j0:e2_row0 + j0 + 8, :]
                            for k in range(i_grp):
                                term = jnp.where(r2t < nrow[k][hh], e2t, 0.0) * crow[k][hh]
                                w[k] = term if w[k] is None else w[k] + term
                        for k in range(i_grp):
                            r0 = sb * PEER_SB + (g0 + k) * NK + j0
                            halves[k][s8] = w[k] * _gelu_tanh(z_scr[lt, r0:r0 + 8, :])
                    for k in range(i_grp):
                        r0 = sb * PEER_SB + (g0 + k) * NK + jt * 16
                        p_scr[lt, r0:r0 + 16, :] = jnp.concatenate(halves[k], axis=0).astype(BF16)

    def stage_out(hf):
        p_h = jnp.concatenate([p_scr[hf * lt_per + l] for l in range(lt_per)], axis=1)
        cs = slice(hf * PEER_CW, (hf + 1) * PEER_CW)
        out_scr[:, cs] += _dot(vt_ref[0], p_h)

    for c in range(len(chunks)):
        stage_z(c)
    for c in range(len(chunks)):
        stage_gate(c)
        if (c + 1) % (EB // PEER_SB) == 0:
            stage_out(chunks[c][1])

    @pl.when(e == n_e - 1)
    def _():
        x2 = x1_ref[...] + mod_ref[0, 0][5:6] * out_scr[...].T
        if final:
            x2 = _rms(x2, D_MODEL) * fg_ref[...]
        o_ref[...] = x2


def _peer_call(h2_all, x1_all, modsel, wq, subk, u_tab, v_tab, fg, n_tok, n_lat, l_len, final):
    n_exp = u_tab.shape[0]
    n_keys = subk.shape[1]
    assert n_exp == n_keys * n_keys and n_keys % 16 == 0
    assert min(PEER_EB, n_exp) % (8 * n_keys) == 0
    assert PEER_SB % n_keys == 0 and min(PEER_EB, n_exp) % PEER_SB == 0
    assert n_tok % PEER_TT == 0 and n_lat % PEER_TT == 0 and l_len % PEER_TT == 0
    TT, EB = PEER_TT, min(PEER_EB, n_exp)
    lat_tiles = n_lat // TT
    per_b = l_len // TT

    def mod_map(i, e):
        return (jnp.where(i < lat_tiles, i // per_b, 0), (i < lat_tiles).astype(jnp.int32), 0, 0)

    n_blk = n_exp // EB
    vt_tab = jnp.transpose(v_tab.reshape(n_blk, EB, D_MODEL), (0, 2, 1))
    kern = functools.partial(_peer_kernel, n_keys=n_keys, final=final)
    full = lambda shape: pl.BlockSpec(shape, lambda i, e: (0,) * len(shape))
    return pl.pallas_call(
        kern,
        grid=(n_tok // TT, n_blk),
        in_specs=[pl.BlockSpec((TT, D_MODEL), lambda i, e: (i, 0)),
                  pl.BlockSpec((TT, D_MODEL), lambda i, e: (i, 0)),
                  pl.BlockSpec((1, 1, 6, D_MODEL), mod_map),
                  full(wq.shape), full(subk.shape),
                  pl.BlockSpec((EB, D_MODEL), lambda i, e: (e, 0)),
                  pl.BlockSpec((1, D_MODEL, EB), lambda i, e: (e, 0, 0)),
                  full((1, D_MODEL))],
        out_specs=pl.BlockSpec((TT, D_MODEL), lambda i, e: (i, 0)),
        out_shape=jax.ShapeDtypeStruct((n_tok, D_MODEL), F32),
        scratch_shapes=[pltpu.VMEM((D_MODEL, TT), BF16),
                        pltpu.VMEM((TT, PEER_HEADS * 256), BF16),
                        pltpu.VMEM((TT // LANES, PEER_HEADS, 2 * (n_keys + PEER_PAD), LANES), F32),
                        pltpu.VMEM((PEER_HEADS, n_keys, TT), F32),
                        pltpu.VMEM((PEER_HEADS, n_keys, TT), F32),
                        pltpu.VMEM((PEER_TOPK, TT), F32),
                        pltpu.VMEM((PEER_TOPK, TT), F32),
                        pltpu.VMEM((TT // LANES, EB, LANES), F32),
                        pltpu.VMEM((TT // LANES, EB, LANES), BF16),
                        pltpu.VMEM((D_MODEL, TT), F32)],
        compiler_params=_cparams(("arbitrary", "arbitrary")),
        name="peer_ffn",
    )(h2_all, x1_all, modsel, wq, subk, u_tab, vt_tab, fg)


def _head_slots(w, n_heads, width=HEAD_DIM):
    lead = w.shape[:-1]
    w = w.reshape(lead + (n_heads, width))
    w = jnp.pad(w, [(0, 0)] * len(lead) + [(0, 0), (0, HEAD_SLOT - width)])
    return w.reshape(lead + (n_heads * HEAD_SLOT,))


def _prep_w_in(w_in):
    sizes = (256, 128, 128, B_Q_LORA, B_KV_LORA, B_ROPE, 256, 256, 256, 256)
    offs = np.cumsum((0,) + sizes)
    aq, ak, av, bq, bkv, bkr, cu, dq, dk, dv = [w_in[:, offs[i]:offs[i + 1]] for i in range(10)]
    d = w_in.shape[0]
    bkr_p = jnp.concatenate([jnp.zeros((d, B_NOPE), F32), bkr, jnp.zeros((d, HEAD_SLOT - B_NOPE - B_ROPE), F32)], 1)
    cols = [_head_slots(aq, 4), _head_slots(ak, 2), _head_slots(av, 2),
            jnp.pad(bq, ((0, 0), (0, 256 - B_Q_LORA))), bkv, bkr_p, cu,
            _head_slots(dq, 4), _head_slots(dk, 4), _head_slots(dv, 4)]
    w = jnp.concatenate(cols, axis=1)
    assert w.shape[1] == _C_END
    return w.astype(BF16)


def _prep_mla(w_uq, w_ukv, qg, kvg):
    wq = _head_slots(w_uq, N_HEADS, B_NOPE + B_ROPE)
    wq = jnp.pad(wq, ((0, 256 - B_Q_LORA), (0, 0))).astype(BF16)
    kv = w_ukv.reshape(B_KV_LORA, N_HEADS, B_NOPE + HEAD_DIM)
    wk = _head_slots(kv[:, :, :B_NOPE].reshape(B_KV_LORA, -1), N_HEADS, B_NOPE)
    wv = _head_slots(kv[:, :, B_NOPE:].reshape(B_KV_LORA, -1), N_HEADS, HEAD_DIM)
    wkv = jnp.concatenate([wk, wv], axis=1).astype(BF16)
    qg_p = jnp.pad(qg, (0, 256 - B_Q_LORA)).reshape(1, 256)
    return wq, wkv, qg_p, kvg.reshape(1, B_KV_LORA)


def _rope_tables(ctx_len, l_len):
    t = np.arange(l_len)
    pos = np.stack([t // GRID_W, t % GRID_W], 0).astype(np.float64)

    def build(width, lane0):
        a = width // 2
        half = a // 2
        inv = ROPE_BASE ** (-np.arange(half, dtype=np.float64) / half)
        cos = np.ones((l_len, LANES)); sp = np.zeros((l_len, LANES)); sm = np.zeros((l_len, LANES))
        for j in range(width):
            axis, i = j // a, j % a
            f, second = i % half, i >= half
            ang = pos[axis] * inv[f]
            cos[:, lane0 + j] = np.cos(ang)
            if second:
                sp[:, lane0 + j] = np.sin(ang)
            else:
                sm[:, lane0 + j] = -np.sin(ang)
        tab = np.stack([cos, sp, sm], 0)
        ctx = np.stack([np.ones((ctx_len, LANES)), np.zeros((ctx_len, LANES)), np.zeros((ctx_len, LANES))], 0)
        return jnp.asarray(np.concatenate([ctx, tab], axis=1), F32)

    return build(HEAD_DIM, 0), build(B_ROPE, B_NOPE)


def _s5_matrices(lam_re, lam_im, log_step, b_re, b_im, c_re, c_im):
    lr, li = lam_re.astype(F32), lam_im.astype(F32)
    step = jnp.exp(log_step.astype(F32))[..., None]
    mag = jnp.exp(lr * step)
    abr, abi = mag * jnp.cos(li * step), mag * jnp.sin(li * step)
    nr, ni = abr - 1.0, abi
    den = lr * lr + li * li
    fr = (nr * lr + ni * li) / den
    fi = (ni * lr - nr * li) / den
    bbr = fr[..., None] * b_re - fi[..., None] * b_im
    bbi = fr[..., None] * b_im + fi[..., None] * b_re
    eye = jnp.eye(C_NGROUPS, dtype=F32)
    bm_r = jnp.einsum('dgpc,gh->dgchp', bbr, eye).reshape(2, GROUP_WIDTH, N_STATE)
    bm_i = jnp.einsum('dgpc,gh->dgchp', bbi, eye).reshape(2, GROUP_WIDTH, N_STATE)
    bmat = jnp.concatenate([bm_r, bm_i], axis=2).astype(BF16)
    cm_r = jnp.einsum('dgcp,gh->dgphc', c_re.astype(F32), eye).reshape(2, N_STATE, GROUP_WIDTH)
    cm_i = jnp.einsum('dgcp,gh->dgphc', c_im.astype(F32), eye).reshape(2, N_STATE, GROUP_WIDTH)
    cmat = jnp.concatenate([cm_r, -cm_i], axis=1).astype(BF16)
    return bmat, cmat, abr.reshape(2, 1, N_STATE), abi.reshape(2, 1, N_STATE)


def _na_bias(rpb):
    w = np.arange(GRID_W)
    cs = np.clip(w - NA_KW // 2, 0, GRID_W - NA_KW)
    c = np.arange(GRID_W)
    inwin = (c[None, :] >= cs[:, None]) & (c[None, :] < cs[:, None] + NA_KW)
    colidx = np.clip(c[None, :] - w[:, None] + NA_KW - 1, 0, 2 * NA_KW - 2)
    col_sel = (colidx[:, :, None] == np.arange(2 * NA_KW - 1)).astype(np.float32)
    d, y = np.arange(NA_KH)[:, None], np.arange(NA_KH)[None, :]
    row_sel = ((y - d + NA_KH - 1)[:, :, None] == np.arange(2 * NA_KH - 1)).astype(np.float32)
    hi = lax.Precision.HIGHEST
    t = jnp.einsum('hrq,wcq->hrwc', rpb.astype(F32), jnp.asarray(col_sel), precision=hi)
    b = jnp.einsum('dyr,hrwc->dhwyc', jnp.asarray(row_sel), t, precision=hi)
    b = jnp.where(jnp.asarray(inwin)[None, None, :, None, :], b, NEG_INF)
    b = b.reshape(NA_KH, N_HEADS, GRID_W, NA_KEYS)
    return jnp.concatenate([b, jnp.full((1, N_HEADS, GRID_W, NA_KEYS), NEG_INF, F32)], axis=0)


def _mix_layout(mix_norm_g, w_out):
    g = mix_norm_g.reshape(4, GROUP_WIDTH)
    w = w_out.reshape(4, GROUP_WIDTH, D_MODEL)
    gs, ws = [], []
    for k in range(4):
        if k == 2:
            gs.append(g[k]); ws.append(w[k])
        else:
            gs.append(_head_slots(g[k], N_HEADS))
            wk = w[k].reshape(N_HEADS, HEAD_DIM, D_MODEL)
            wk = jnp.pad(wk, ((0, 0), (0, HEAD_SLOT - HEAD_DIM), (0, 0))).reshape(N_HEADS * HEAD_SLOT, D_MODEL)
            ws.append(wk)
    return jnp.concatenate(gs).reshape(1, -1), jnp.concatenate(ws, axis=0).astype(BF16)


def kernel(x, c, ctx, c_ctx, norm1_g, norm2_g, w_ada, b_ada, w_in, swa_sink, mla_q_norm_g, mla_w_uq,
           mla_kv_norm_g, mla_w_ukv, s5_lambda_re, s5_lambda_im, s5_log_step, s5_b_re, s5_b_im, s5_c_re,
           s5_c_im, s5_d, s5_w_glu, na_rpb, mix_norm_g, w_out, peer_w_q, peer_sub_keys, peer_u, peer_v,
           final_norm_g):
    n_b, l_len, d = x.shape
    ctx_len = ctx.shape[1]
    depth = w_in.shape[0]
    s_len = ctx_len + l_len
    assert d == D_MODEL and ctx_len == TOKEN_TILE and n_b % 8 == 0 and n_b < 16
    assert l_len // GRID_W >= NA_KH and l_len % PEER_TT == 0
    n_lat = n_b * l_len

    cc = jnp.concatenate([c, c_ctx[None], jnp.zeros((16 - n_b - 1, d), F32)], axis=0)
    mod = _ada_call(cc, w_ada, b_ada).reshape(depth, 16, 6, d)
    lat_mod = mod[:, :n_b]
    ctx_mod = jnp.broadcast_to(mod[:, n_b:n_b + 1], lat_mod.shape)
    modsel = jnp.stack([ctx_mod, lat_mod], axis=2)

    tab_a, tab_b = _rope_tables(ctx_len, l_len)
    x_all = jnp.concatenate([x.reshape(n_lat, d), ctx.reshape(n_b * ctx_len, d)], axis=0)
    fg = final_norm_g.reshape(1, d)

    for l in range(depth):
        last = l == depth - 1
        w_in_p = _prep_w_in(w_in[l])
        wuq, wukv, qg, kvg = _prep_mla(mla_w_uq[l], mla_w_ukv[l], mla_q_norm_g[l], mla_kv_norm_g[l])
        qa, ka, va, qb, kb, vb, cu, qd, kd, vd = _proj_call(
            x_all, modsel[l], norm1_g[l].reshape(1, d), w_in_p, tab_a, tab_b, qg, wuq, kvg, wukv, n_b, s_len)

        ya = _swa_call(swa_sink[l], qa, ka, va, ctx_len)
        yb = _mla_call(qb, kb, vb, ctx_len)

        bmat, cmat, a_r, a_i = _s5_matrices(s5_lambda_re[l], s5_lambda_im[l], s5_log_step[l],
                                            s5_b_re[l], s5_b_im[l], s5_c_re[l], s5_c_im[l])
        cu_t = cu.reshape(s_len, n_b, GROUP_WIDTH)
        y_dirs = _s5_call(cu_t, bmat, cmat, a_r, a_i, ctx_len)
        ys_t = _s5_glu_call(cu.reshape(s_len * n_b, GROUP_WIDTH), y_dirs.reshape(2, s_len * n_b, GROUP_WIDTH),
                            s5_d[l].reshape(1, GROUP_WIDTH), s5_w_glu[l].astype(BF16))
        ys = ys_t.reshape(s_len, n_b * GROUP_WIDTH)

        yd = _na_call(qd, kd, vd, _na_bias(na_rpb[l]), ctx_len)

        mixg_p, w_out_p = _mix_layout(mix_norm_g[l], w_out[l])
        x1_all, h2_all = _out_call(ya, yb, ys, yd, x_all, modsel[l], mixg_p, w_out_p, norm2_g[l].reshape(1, d))

        n_tok = n_lat if last else x_all.shape[0]
        x_all = _peer_call(h2_all, x1_all, modsel[l], peer_w_q[l].astype(BF16), peer_sub_keys[l].astype(BF16),
                           peer_u[l].astype(BF16), peer_v[l].astype(BF16), fg, n_tok, n_lat, l_len, last)

    return x_all[:n_lat].reshape(n_b, l_len, d)
```

```python
import functools
import math

import numpy as np
import jax
import jax.numpy as jnp
from jax import lax
from jax.experimental import pallas as pl
from jax.experimental.pallas import tpu as pltpu

F32 = jnp.float32
BF16 = jnp.bfloat16

D_MODEL = 1024
HEAD_DIM = 64
GROUP_WIDTH = 256
GRID_W = 64
EPS = 1e-6
NEG_INF = -1e30
ROPE_BASE = 10000.0
N_HEADS = 4
A_KV_HEADS = 2
A_WINDOW = 128
A_BLOCK = 128
B_NOPE = 64
B_ROPE = 32
B_Q_LORA = 192
B_KV_LORA = 128
C_GROUP = 16
C_NGROUPS = 16
C_STATE = 64
NA_KH = 8
NA_KW = 16
PEER_HEADS = 8
PEER_TOPK = 16

LANES = 128
HEAD_SLOT = LANES
TOKEN_TILE = 256
VMEM_LIMIT = 60 * 1024 * 1024

N_STATE = C_NGROUPS * C_STATE


def _cparams(sem):
    return pltpu.CompilerParams(dimension_semantics=sem, vmem_limit_bytes=VMEM_LIMIT)


def _nt_dot(a, b):
    return lax.dot_general(a, b, (((1,), (1,)), ((), ())), preferred_element_type=F32)


def _dot(a, b):
    return jnp.dot(a, b, preferred_element_type=F32)


def _ada_kernel(c_ref, w_ref, b_ref, o_ref):
    c = c_ref[...]
    s = c * jax.nn.sigmoid(c)
    o_ref[0] = _dot(s.astype(BF16), w_ref[0].astype(BF16)) + b_ref[0]


def _ada_call(cc, w_ada, b_ada):
    depth, d, n6 = w_ada.shape
    tn = 1536
    return pl.pallas_call(
        _ada_kernel,
        grid=(depth, n6 // tn),
        in_specs=[pl.BlockSpec((cc.shape[0], d), lambda l, j: (0, 0)),
                  pl.BlockSpec((1, d, tn), lambda l, j: (l, 0, j)),
                  pl.BlockSpec((1, 1, tn), lambda l, j: (l, 0, j))],
        out_specs=pl.BlockSpec((1, cc.shape[0], tn), lambda l, j: (l, 0, j)),
        out_shape=jax.ShapeDtypeStruct((depth, cc.shape[0], n6), F32),
        compiler_params=_cparams(("arbitrary", "arbitrary")),
        name="ada_mod",
    )(cc, w_ada, b_ada.reshape(depth, 1, n6))


_C_AQ = 0
_C_AK = _C_AQ + 4 * HEAD_SLOT
_C_AV = _C_AK + 2 * HEAD_SLOT
_C_BQ = _C_AV + 2 * HEAD_SLOT
_C_BKV = _C_BQ + 256
_C_BKR = _C_BKV + 128
_C_CU = _C_BKR + 128
_C_DQ = _C_CU + 256
_C_DK = _C_DQ + 4 * HEAD_SLOT
_C_DV = _C_DK + 4 * HEAD_SLOT
_C_END = _C_DV + 4 * HEAD_SLOT


def _rms(x, n):
    return x * lax.rsqrt(jnp.sum(x * x, axis=-1, keepdims=True) * (1.0 / n) + EPS)


def _rope(t, tab_ref, shift):
    return (t * tab_ref[0]
            + pltpu.roll(t, shift, 1) * tab_ref[1]
            + pltpu.roll(t, LANES - shift, 1) * tab_ref[2])


def _proj_kernel(x_ref, mod_ref, g_ref, w_ref, ta_ref, tb_ref, qg_ref, wuq_ref, kvg_ref, wukv_ref,
                 qa_ref, ka_ref, va_ref, qb_ref, kb_ref, vb_ref, cu_ref, qd_ref, kd_ref, vd_ref):
    x = x_ref[...]
    m = mod_ref[0, 0]
    h = _rms(x, D_MODEL) * g_ref[...]
    h = h * (1.0 + m[1:2]) + m[0:1]
    z = _dot(h.astype(BF16), w_ref[...])

    a_scale = HEAD_DIM ** -0.5
    for hh in range(N_HEADS):
        c0 = _C_AQ + hh * HEAD_SLOT
        qa_ref[0, :, hh * HEAD_SLOT:(hh + 1) * HEAD_SLOT] = (
            _rope(z[:, c0:c0 + HEAD_SLOT], ta_ref, 16) * a_scale).astype(BF16)
    for g in range(A_KV_HEADS):
        c0 = _C_AK + g * HEAD_SLOT
        ka_ref[0, :, g * HEAD_SLOT:(g + 1) * HEAD_SLOT] = _rope(z[:, c0:c0 + HEAD_SLOT], ta_ref, 16).astype(BF16)
    va_ref[0] = z[:, _C_AV:_C_AV + 2 * HEAD_SLOT].astype(BF16)

    cq = _rms(z[:, _C_BQ:_C_BQ + 256], B_Q_LORA) * qg_ref[...]
    qb = _dot(cq.astype(BF16), wuq_ref[...])
    ckv = _rms(z[:, _C_BKV:_C_BKV + 128], B_KV_LORA) * kvg_ref[...]
    kv = _dot(ckv.astype(BF16), wukv_ref[...])
    kr = _rope(z[:, _C_BKR:_C_BKR + 128], tb_ref, 8)
    lane = lax.broadcasted_iota(jnp.int32, (1, HEAD_SLOT), 1)
    ones_col = (lane == HEAD_DIM).astype(F32)
    for hh in range(N_HEADS):
        sl = slice(hh * HEAD_SLOT, (hh + 1) * HEAD_SLOT)
        qb_ref[0, :, sl] = _rope(qb[:, sl], tb_ref, 8).astype(BF16)
        kb_ref[0, :, sl] = (kv[:, sl] + kr).astype(BF16)
        vsl = slice(4 * HEAD_SLOT + hh * HEAD_SLOT, 4 * HEAD_SLOT + (hh + 1) * HEAD_SLOT)
        vb_ref[0, :, sl] = (kv[:, vsl] + ones_col).astype(BF16)

    cu_ref[...] = z[:, _C_CU:_C_CU + 256]
    qd_ref[0] = (z[:, _C_DQ:_C_DQ + 4 * HEAD_SLOT] * a_scale).astype(BF16)
    kd_ref[0] = z[:, _C_DK:_C_DK + 4 * HEAD_SLOT].astype(BF16)
    vd_ref[0] = z[:, _C_DV:_C_DV + 4 * HEAD_SLOT].astype(BF16)


def _stream_block(b, t, n_b, lat_tiles):
    return jnp.where(t == 0, n_b * lat_tiles + b, b * lat_tiles + t - 1)


def _proj_call(x_all, modsel, g1, w_in_p, tab_a, tab_b, qg, wuq, kvg, wukv, n_b, s_len):
    tiles = s_len // TOKEN_TILE
    lat_tiles = tiles - 1
    T = TOKEN_TILE
    xmap = lambda b, t: (_stream_block(b, t, n_b, lat_tiles), 0)
    full = lambda shape: pl.BlockSpec(shape, lambda b, t: (0,) * len(shape))
    omap = lambda b, t: (b, t, 0)
    widths = [4 * HEAD_SLOT, 2 * HEAD_SLOT, 2 * HEAD_SLOT, 4 * HEAD_SLOT, 4 * HEAD_SLOT, 4 * HEAD_SLOT,
              256, 4 * HEAD_SLOT, 4 * HEAD_SLOT, 4 * HEAD_SLOT]
    dts = [BF16, BF16, BF16, BF16, BF16, BF16, F32, BF16, BF16, BF16]
    return pl.pallas_call(
        _proj_kernel,
        grid=(n_b, tiles),
        in_specs=[pl.BlockSpec((T, D_MODEL), xmap),
                  pl.BlockSpec((1, 1, 6, D_MODEL), lambda b, t: (b, jnp.minimum(t, 1), 0, 0)),
                  full((1, D_MODEL)),
                  full(w_in_p.shape),
                  pl.BlockSpec((3, T, LANES), lambda b, t: (0, t, 0)),
                  pl.BlockSpec((3, T, LANES), lambda b, t: (0, t, 0)),
                  full(qg.shape), full(wuq.shape), full(kvg.shape), full(wukv.shape)],
        out_specs=[pl.BlockSpec((T, w), lambda b, t: (t, b)) if i == 6 else pl.BlockSpec((1, T, w), omap)
                   for i, w in enumerate(widths)],
        out_shape=[jax.ShapeDtypeStruct((s_len, n_b * w) if i == 6 else (n_b, s_len, w), dt)
                   for i, (w, dt) in enumerate(zip(widths, dts))],
        compiler_params=_cparams(("arbitrary", "arbitrary")),
        name="mod_proj",
    )(x_all, modsel, g1, w_in_p, tab_a, tab_b, qg, wuq, kvg, wukv)


def _swa_kernel(sink_ref, q_ref, k_ref, v_ref, o_ref, *, ctx_len, s_len):
    n = pl.program_id(1)
    is_lat = n >= ctx_len // A_BLOCK
    band = 3 * A_BLOCK
    ks = pl.multiple_of(jnp.clip((n - 1) * A_BLOCK, 0, s_len - band), A_BLOCK)
    q = q_ref[0]
    kl = k_ref[0, pl.ds(ks, band), :]
    vl = v_ref[0, pl.ds(ks, band), :]
    kc = k_ref[0, 0:ctx_len, :]
    vc = v_ref[0, 0:ctx_len, :]
    qpos = n * A_BLOCK + lax.broadcasted_iota(jnp.int32, (A_BLOCK, band), 0)
    kpos = ks + lax.broadcasted_iota(jnp.int32, (A_BLOCK, band), 1)
    window = jnp.where(is_lat, A_WINDOW, -1)
    valid = (jnp.abs(qpos - kpos) <= window) & (kpos >= ctx_len)
    rep = N_HEADS // A_KV_HEADS
    scores = []
    for hh in range(N_HEADS):
        qs = slice(hh * HEAD_SLOT, (hh + 1) * HEAD_SLOT)
        gs = slice((hh // rep) * HEAD_SLOT, (hh // rep + 1) * HEAD_SLOT)
        scores.append((_nt_dot(q[:, qs], kl[:, gs]), _nt_dot(q[:, qs], kc[:, gs])))
    for hh in range(N_HEADS):
        g = hh // rep
        qs = slice(hh * HEAD_SLOT, (hh + 1) * HEAD_SLOT)
        gs = slice(g * HEAD_SLOT, (g + 1) * HEAD_SLOT)
        s_loc = jnp.where(valid, scores[hh][0], NEG_INF)
        s_ctx = scores[hh][1]
        sk = sink_ref[hh]
        mx = jnp.maximum(jnp.max(s_loc, axis=-1, keepdims=True), jnp.max(s_ctx, axis=-1, keepdims=True))
        mx = jnp.maximum(mx, sk)
        p_loc = jnp.exp(s_loc - mx)
        p_ctx = jnp.exp(s_ctx - mx)
        den = (jnp.sum(p_loc, axis=-1, keepdims=True) + jnp.sum(p_ctx, axis=-1, keepdims=True)
               + jnp.exp(sk - mx))
        o = _dot(p_loc.astype(BF16), vl[:, gs]) + _dot(p_ctx.astype(BF16), vc[:, gs])
        o_ref[0, :, qs] = o * (1.0 / den)


def _swa_call(sink, qa, ka, va, ctx_len):
    n_b, s_len, _ = qa.shape
    kern = functools.partial(_swa_kernel, ctx_len=ctx_len, s_len=s_len)
    return pl.pallas_call(
        kern,
        grid=(n_b, s_len // A_BLOCK),
        in_specs=[pl.BlockSpec(memory_space=pltpu.SMEM),
                  pl.BlockSpec((1, A_BLOCK, 4 * HEAD_SLOT), lambda b, n: (b, n, 0)),
                  pl.BlockSpec((1, s_len, 2 * HEAD_SLOT), lambda b, n: (b, 0, 0)),
                  pl.BlockSpec((1, s_len, 2 * HEAD_SLOT), lambda b, n: (b, 0, 0))],
        out_specs=pl.BlockSpec((1, A_BLOCK, 4 * HEAD_SLOT), lambda b, n: (b, n, 0)),
        out_shape=jax.ShapeDtypeStruct((n_b, s_len, 4 * HEAD_SLOT), F32),
        compiler_params=_cparams(("arbitrary", "arbitrary")),
        name="swa_mixer",
    )(sink, qa, ka, va)


MLA_TQ = 256


def _mla_kernel(q_ref, k_ref, v_ref, o_ref, *, ctx_len, s_len):
    t = pl.program_id(1)
    is_lat = t >= ctx_len // MLA_TQ
    scale = (B_NOPE + B_ROPE) ** -0.5
    lane = lax.broadcasted_iota(jnp.int32, (1, HEAD_SLOT), 1)

    def attend(n_keys):
        def scores(hh):
            sl = slice(hh * HEAD_SLOT, (hh + 1) * HEAD_SLOT)
            return _nt_dot(q_ref[0, :, sl], k_ref[0, 0:n_keys, sl]) * scale

        s_next = scores(0)
        for hh in range(N_HEADS):
            sl = slice(hh * HEAD_SLOT, (hh + 1) * HEAD_SLOT)
            s = s_next
            if hh + 1 < N_HEADS:
                s_next = scores(hh + 1)
            p = jnp.exp(s - jnp.max(s, axis=-1, keepdims=True))
            acc = _dot(p.astype(BF16), v_ref[0, 0:n_keys, sl])
            den = acc[:, HEAD_DIM:HEAD_DIM + 1]
            o_ref[0, :, sl] = jnp.where(lane < HEAD_DIM, acc * (1.0 / den), 0.0)

    @pl.when(is_lat)
    def _():
        attend(s_len)

    @pl.when(jnp.logical_not(is_lat))
    def _():
        attend(ctx_len)


def _mla_call(qb, kb, vb, ctx_len):
    n_b, s_len, _ = qb.shape
    assert ctx_len % MLA_TQ == 0 and s_len % MLA_TQ == 0
    kern = functools.partial(_mla_kernel, ctx_len=ctx_len, s_len=s_len)
    return pl.pallas_call(
        kern,
        grid=(n_b, s_len // MLA_TQ),
        in_specs=[pl.BlockSpec((1, MLA_TQ, 4 * HEAD_SLOT), lambda b, t: (b, t, 0)),
                  pl.BlockSpec((1, s_len, 4 * HEAD_SLOT), lambda b, t: (b, 0, 0)),
                  pl.BlockSpec((1, s_len, 4 * HEAD_SLOT), lambda b, t: (b, 0, 0))],
        out_specs=pl.BlockSpec((1, MLA_TQ, 4 * HEAD_SLOT), lambda b, t: (b, t, 0)),
        out_shape=jax.ShapeDtypeStruct((n_b, s_len, 4 * HEAD_SLOT), F32),
        compiler_params=_cparams(("arbitrary", "arbitrary")),
        name="mla_mixer",
    )(qb, kb, vb)


S5_T = 128


def _s5_kernel(u_ref, bm_ref, cm_ref, ar_ref, ai_ref, y_ref, hr_scr, hi_scr, bu_scr, *, n_b):
    d = pl.program_id(0)
    c = pl.program_id(1)

    @pl.when(c == 0)
    def _():
        hr_scr[...] = jnp.zeros_like(hr_scr)
        hi_scr[...] = jnp.zeros_like(hi_scr)

    u = u_ref[...].reshape(S5_T * n_b, GROUP_WIDTH)
    bu_scr[...] = _dot(u.astype(BF16), bm_ref[0])
    a_r = jnp.broadcast_to(ar_ref[0], (n_b, N_STATE))
    a_i = jnp.broadcast_to(ai_ref[0], (n_b, N_STATE))

    def step(i, carry):
        h_r, h_i = carry
        t = jnp.where(d == 0, i, S5_T - 1 - i)
        row = pl.multiple_of(t * n_b, n_b)
        b_r = bu_scr[pl.ds(row, n_b), 0:N_STATE]
        b_i = bu_scr[pl.ds(row, n_b), N_STATE:2 * N_STATE]
        n_r = a_r * h_r - a_i * h_i + b_r
        n_i = a_r * h_i + a_i * h_r + b_i
        bu_scr[pl.ds(row, n_b), 0:N_STATE] = n_r
        bu_scr[pl.ds(row, n_b), N_STATE:2 * N_STATE] = n_i
        return n_r, n_i

    h_r, h_i = lax.fori_loop(0, S5_T, step, (hr_scr[...], hi_scr[...]), unroll=4)
    hr_scr[...] = h_r
    hi_scr[...] = h_i
    y = _dot(bu_scr[...].astype(BF16), cm_ref[0])
    y_ref[0] = y.reshape(S5_T, n_b, GROUP_WIDTH)


def _s5_chunk(d, c, n_chunks, ctx_chunks):
    rev = jnp.where(c < ctx_chunks, ctx_chunks - 1 - c, n_chunks - 1 - (c - ctx_chunks))
    return jnp.where(d == 0, c, rev)


def _s5_call(cu_t, bmat, cmat, a_r, a_i, ctx_len):
    s_len, n_b, _ = cu_t.shape
    n_chunks = s_len // S5_T
    ctx_chunks = ctx_len // S5_T
    cmap = lambda d, c: (_s5_chunk(d, c, n_chunks, ctx_chunks), 0, 0)
    kern = functools.partial(_s5_kernel, n_b=n_b)
    return pl.pallas_call(
        kern,
        grid=(2, n_chunks),
        in_specs=[pl.BlockSpec((S5_T, n_b, GROUP_WIDTH), cmap),
                  pl.BlockSpec((1, GROUP_WIDTH, 2 * N_STATE), lambda d, c: (d, 0, 0)),
                  pl.BlockSpec((1, 2 * N_STATE, GROUP_WIDTH), lambda d, c: (d, 0, 0)),
                  pl.BlockSpec((1, 1, N_STATE), lambda d, c: (d, 0, 0)),
                  pl.BlockSpec((1, 1, N_STATE), lambda d, c: (d, 0, 0))],
        out_specs=pl.BlockSpec((1, S5_T, n_b, GROUP_WIDTH),
                               lambda d, c: (d, _s5_chunk(d, c, n_chunks, ctx_chunks), 0, 0)),
        out_shape=jax.ShapeDtypeStruct((2, s_len, n_b, GROUP_WIDTH), F32),
        scratch_shapes=[pltpu.VMEM((n_b, N_STATE), F32), pltpu.VMEM((n_b, N_STATE), F32),
                        pltpu.VMEM((S5_T * n_b, 2 * N_STATE), F32)],
        compiler_params=_cparams(("arbitrary", "arbitrary")),
        name="s5_scan",
    )(cu_t, bmat, cmat, a_r, a_i)


def _s5_glu_kernel(u_ref, y_ref, d_ref, w_ref, o_ref):
    y = u_ref[...] * d_ref[...] + y_ref[0] + y_ref[1]
    z = _dot(jax.nn.gelu(y).astype(BF16), w_ref[...])
    o_ref[...] = z[:, :GROUP_WIDTH] * jax.nn.sigmoid(z[:, GROUP_WIDTH:])


def _s5_glu_call(cu_flat, y_dirs, d_skip, w_glu):
    n = cu_flat.shape[0]
    tr = 1024
    return pl.pallas_call(
        _s5_glu_kernel,
        grid=(n // tr,),
        in_specs=[pl.BlockSpec((tr, GROUP_WIDTH), lambda i: (i, 0)),
                  pl.BlockSpec((2, tr, GROUP_WIDTH), lambda i: (0, i, 0)),
                  pl.BlockSpec((1, GROUP_WIDTH), lambda i: (0, 0)),
                  pl.BlockSpec((GROUP_WIDTH, 2 * GROUP_WIDTH), lambda i: (0, 0))],
        out_specs=pl.BlockSpec((tr, GROUP_WIDTH), lambda i: (i, 0)),
        out_shape=jax.ShapeDtypeStruct((n, GROUP_WIDTH), F32),
        compiler_params=_cparams(("arbitrary",)),
        name="s5_glu",
    )(cu_flat, y_dirs, d_skip, w_glu)


NA_KEYS = NA_KH * GRID_W


NA_RB = 4


def _na_kernel(q_ref, k_ref, v_ref, bias_ref, o_ref, *, ctx_len, n_rows):
    j = pl.program_id(1)
    ctx_steps = ctx_len // (NA_RB * GRID_W)
    kc = k_ref[0, 0:ctx_len, :]
    vc = v_ref[0, 0:ctx_len, :]
    scores = []
    for i in range(NA_RB):
        r = (j - ctx_steps) * NA_RB + i
        r0 = jnp.clip(r - NA_KH // 2, 0, n_rows - NA_KH)
        var = jnp.where(j < ctx_steps, NA_KH, r - r0)
        ks = pl.multiple_of(ctx_len + r0 * GRID_W, GRID_W)
        q = q_ref[0, i * GRID_W:(i + 1) * GRID_W, :]
        kl = k_ref[0, pl.ds(ks, NA_KEYS), :]
        for hh in range(N_HEADS):
            sl = slice(hh * HEAD_SLOT, (hh + 1) * HEAD_SLOT)
            scores.append((_nt_dot(q[:, sl], kl[:, sl]) + bias_ref[var, hh], _nt_dot(q[:, sl], kc[:, sl]), ks))
    for i in range(NA_RB):
        qs = slice(i * GRID_W, (i + 1) * GRID_W)
        for hh in range(N_HEADS):
            sl = slice(hh * HEAD_SLOT, (hh + 1) * HEAD_SLOT)
            s_loc, s_ctx, ks = scores[i * N_HEADS + hh]
            mx = jnp.maximum(jnp.max(s_loc, axis=-1, keepdims=True), jnp.max(s_ctx, axis=-1, keepdims=True))
            p_loc = jnp.exp(s_loc - mx)
            p_ctx = jnp.exp(s_ctx - mx)
            den = jnp.sum(p_loc, axis=-1, keepdims=True) + jnp.sum(p_ctx, axis=-1, keepdims=True)
            o = (_dot(p_loc.astype(BF16), v_ref[0, pl.ds(ks, NA_KEYS), sl])
                 + _dot(p_ctx.astype(BF16), vc[:, sl]))
            o_ref[0, qs, sl] = o * (1.0 / den)


def _na_call(qd, kd, vd, bias, ctx_len):
    n_b, s_len, _ = qd.shape
    n_rows = (s_len - ctx_len) // GRID_W
    blk = NA_RB * GRID_W
    assert ctx_len % blk == 0 and n_rows % NA_RB == 0
    kern = functools.partial(_na_kernel, ctx_len=ctx_len, n_rows=n_rows)
    return pl.pallas_call(
        kern,
        grid=(n_b, s_len // blk),
        in_specs=[pl.BlockSpec((1, blk, 4 * HEAD_SLOT), lambda b, j: (b, j, 0)),
                  pl.BlockSpec((1, s_len, 4 * HEAD_SLOT), lambda b, j: (b, 0, 0)),
                  pl.BlockSpec((1, s_len, 4 * HEAD_SLOT), lambda b, j: (b, 0, 0)),
                  pl.BlockSpec(bias.shape, lambda b, j: (0, 0, 0, 0))],
        out_specs=pl.BlockSpec((1, blk, 4 * HEAD_SLOT), lambda b, j: (b, j, 0)),
        out_shape=jax.ShapeDtypeStruct((n_b, s_len, 4 * HEAD_SLOT), F32),
        compiler_params=_cparams(("arbitrary", "arbitrary")),
        name="na_mixer",
    )(qd, kd, vd, bias)


def _out_kernel(ya_ref, yb_ref, ys_ref, yd_ref, x_ref, mod_ref, mg_ref, w_ref, g2_ref, x1_ref, h2_ref):
    m = mod_ref[0, 0]
    mg = mg_ref[...]
    parts = []
    off = 0
    for ref, width in ((ya_ref, 4 * HEAD_SLOT), (yb_ref, 4 * HEAD_SLOT), (ys_ref, GROUP_WIDTH),
                       (yd_ref, 4 * HEAD_SLOT)):
        y_k = ref[...] if ref is ys_ref else ref[0]
        parts.append((_rms(y_k, GROUP_WIDTH) * mg[:, off:off + width]).astype(BF16))
        off += width
    y = _dot(jnp.concatenate(parts, axis=-1), w_ref[...])
    x1 = x_ref[...] + m[2:3] * y
    x1_ref[...] = x1
    h2 = _rms(x1, D_MODEL) * g2_ref[...]
    h2_ref[...] = (h2 * (1.0 + m[4:5]) + m[3:4]).astype(BF16)


def _out_call(ya, yb, ys, yd, x_all, modsel, mixg_p, w_out_p, g2):
    n_b, s_len, _ = ya.shape
    tiles = s_len // TOKEN_TILE
    lat_tiles = tiles - 1
    T = TOKEN_TILE
    xmap = lambda b, t: (_stream_block(b, t, n_b, lat_tiles), 0)
    full = lambda shape: pl.BlockSpec(shape, lambda b, t: (0,) * len(shape))
    ymap = lambda b, t: (b, t, 0)
    return pl.pallas_call(
        _out_kernel,
        grid=(n_b, tiles),
        in_specs=[pl.BlockSpec((1, T, ya.shape[-1]), ymap), pl.BlockSpec((1, T, yb.shape[-1]), ymap),
                  pl.BlockSpec((T, GROUP_WIDTH), lambda b, t: (t, b)),
                  pl.BlockSpec((1, T, yd.shape[-1]), ymap),
                  pl.BlockSpec((T, D_MODEL), xmap),
                  pl.BlockSpec((1, 1, 6, D_MODEL), lambda b, t: (b, jnp.minimum(t, 1), 0, 0)),
                  full(mixg_p.shape), full(w_out_p.shape), full((1, D_MODEL))],
        out_specs=[pl.BlockSpec((T, D_MODEL), xmap), pl.BlockSpec((T, D_MODEL), xmap)],
        out_shape=[jax.ShapeDtypeStruct(x_all.shape, F32), jax.ShapeDtypeStruct(x_all.shape, BF16)],
        compiler_params=_cparams(("arbitrary", "arbitrary")),
        name="out_proj",
    )(ya, yb, ys, yd, x_all, modsel, mixg_p, w_out_p, g2)


PEER_TT = 512
PEER_EB = 2048
PEER_SB = 512
PEER_CW = 512
PEER_IGRP = 2
PEER_PAD = 8


_GELU_B = -2.0 * math.sqrt(2.0 / math.pi) * math.log2(math.e)
_GELU_A = 0.044715 * _GELU_B


def _gelu_tanh(x):
    e = jnp.exp2(x * (x * x * _GELU_A + _GELU_B))
    return x * (1.0 / (1.0 + e))


def _peer_kernel(h2_ref, x1_ref, mod_ref, wq_ref, sk_ref, u_ref, vt_ref, fg_ref, o_ref,
                 h2t_scr, q_scr, g_scr, n_scr, c1_scr, v1_scr, v2_scr, z_scr, p_scr, out_scr,
                 *, n_keys, final):
    e = pl.program_id(1)
    n_e = pl.num_programs(1)
    TT = PEER_TT
    LT = TT // LANES
    NK = n_keys
    e2_row0 = NK + PEER_PAD
    neg = -jnp.inf

    @pl.when(e == 0)
    def _():
        h2 = h2_ref[...]
        h2t_scr[...] = h2.astype(F32).T.astype(BF16)
        q_scr[...] = _dot(h2, wq_ref[...]).astype(BF16)
        out_scr[...] = jnp.zeros_like(out_scr)

        def extract(s, v_scr, ls, want_rank):
            w = s
            rank = jnp.full(s.shape, 127.0, F32) if want_rank else None
            for k in range(PEER_TOPK):
                mk = jnp.max(w, axis=0, keepdims=True)
                v_scr[k:k + 1, ls] = mk
                hit = w == mk
                if want_rank:
                    rank = jnp.where(hit, float(k), rank)
                if k + 1 < PEER_TOPK:
                    w = jnp.where(hit, neg, w)
            return rank

        def head(hh, carry):
            c0 = pl.multiple_of(hh * 256, 256)
            s1_all = _nt_dot(sk_ref[0], q_scr[:, pl.ds(c0, 128)])
            s2_all = _nt_dot(sk_ref[1], q_scr[:, pl.ds(c0 + 128, 128)])
            for lt in range(LT):
                ls = slice(lt * LANES, (lt + 1) * LANES)
                s1 = s1_all[:, ls]
                s2 = s2_all[:, ls]
                extract(s1, v1_scr, ls, False)
                r2 = extract(s2, v2_scr, ls, True)
                slabs = [v1_scr[0:1, ls] + v2_scr[0:16, ls]]
                for a in range(1, 5):
                    slabs.append(v1_scr[a:a + 1, ls] + v2_scr[0:8, ls])
                slabs.append(v1_scr[0:8, ls] + v2_scr[1:2, ls])
                slabs.append(v1_scr[0:8, ls] + v2_scr[0:1, ls])
                slabs.append(v1_scr[8:16, ls] + v2_scr[0:1, ls])
                cand = jnp.concatenate(slabs, axis=0)
                top = v1_scr[0:1, ls] + v2_scr[0:1, ls]
                zsum = jnp.zeros((1, LANES), F32)
                tau = top
                for k in range(PEER_TOPK):
                    tau = jnp.max(cand, axis=0, keepdims=True)
                    zsum = zsum + jnp.exp(tau - top)
                    if k + 1 < PEER_TOPK:
                        cand = jnp.where(cand == tau, neg, cand)
                count = jnp.zeros(s1.shape, F32)
                for a in range(PEER_TOPK):
                    v1a = v1_scr[a:a + 1, ls]
                    sel = (v1a + v2_scr[0:16, ls]) >= tau
                    cnt = jnp.sum(jnp.where(sel, 1.0, 0.0), axis=0, keepdims=True)
                    count = jnp.where(s1 == v1a, cnt, count)
                n_scr[hh, :, ls] = count
                c1_scr[hh, :, ls] = jnp.exp(s1 - v1_scr[0:1, ls]) * (1.0 / zsum)
                g_scr[lt, hh, 0:NK, :] = r2
                g_scr[lt, hh, e2_row0:e2_row0 + NK, :] = jnp.exp(s2 - v2_scr[0:1, ls])
            return carry

        lax.fori_loop(0, PEER_HEADS, head, 0)

    EB = u_ref.shape[0]
    i_per = EB // NK
    i_per_sb = PEER_SB // NK
    jt_n = NK // 16
    lt_per = PEER_CW // LANES
    n_half = TT // PEER_CW
    chunks = [(sb, hf) for hf in range(n_half) for sb in range(EB // PEER_SB)]
    i_grp = min(PEER_IGRP, i_per_sb)

    def stage_z(c):
        sb, hf = chunks[c]
        rows = slice(sb * PEER_SB, (sb + 1) * PEER_SB)
        z = _dot(u_ref[rows, :], h2t_scr[:, hf * PEER_CW:(hf + 1) * PEER_CW])
        for l in range(lt_per):
            z_scr[hf * lt_per + l, rows, :] = z[:, l * LANES:(l + 1) * LANES]

    def stage_gate(c):
        sb, hf = chunks[c]
        for lt in range(hf * lt_per, (hf + 1) * lt_per):
            ls = slice(lt * LANES, (lt + 1) * LANES)
            for g0 in range(0, i_per_sb, i_grp):
                nrow, crow = [], []
                for i2 in range(g0, g0 + i_grp):
                    ii = sb * i_per_sb + i2
                    i8 = pl.multiple_of(e * i_per + (ii // 8) * 8, 8)
                    r8 = slice(ii % 8, ii % 8 + 1)
                    nrow.append([jnp.broadcast_to(n_scr[hh, pl.ds(i8, 8), ls][r8], (8, LANES))
                                 for hh in range(PEER_HEADS)])
                    crow.append([jnp.broadcast_to(c1_scr[hh, pl.ds(i8, 8), ls][r8], (8, LANES))
                                 for hh in range(PEER_HEADS)])
                for jt in range(jt_n):
                    halves = [[None, None] for _ in range(i_grp)]
                    for s8 in range(2):
                        j0 = jt * 16 + s8 * 8
                        w = [None] * i_grp
                        for hh in range(PEER_HEADS):
                            r2t = g_scr[lt, hh, j0:j0 + 8, :]
                            e2t = g_scr[lt, hh, e2_row0 + j0:e2_row0 + j0 + 8, :]
                            for k in range(i_grp):
                                term = jnp.where(r2t < nrow[k][hh], e2t, 0.0) * crow[k][hh]
                                w[k] = term if w[k] is None else w[k] + term
                        for k in range(i_grp):
                            r0 = sb * PEER_SB + (g0 + k) * NK + j0
                            halves[k][s8] = w[k] * _gelu_tanh(z_scr[lt, r0:r0 + 8, :])
                    for k in range(i_grp):
                        r0 = sb * PEER_SB + (g0 + k) * NK + jt * 16
                        p_scr[lt, r0:r0 + 16, :] = jnp.concatenate(halves[k], axis=0).astype(BF16)

    def stage_out(hf):
        p_h = jnp.concatenate([p_scr[hf * lt_per + l] for l in range(lt_per)], axis=1)
        cs = slice(hf * PEER_CW, (hf + 1) * PEER_CW)
        out_scr[:, cs] += _dot(vt_ref[0], p_h)

    for c in range(len(chunks)):
        stage_z(c)
    for c in range(len(chunks)):
        stage_gate(c)
        if (c + 1) % (EB // PEER_SB) == 0:
            stage_out(chunks[c][1])

    @pl.when(e == n_e - 1)
    def _():
        x2 = x1_ref[...] + mod_ref[0, 0][5:6] * out_scr[...].T
        if final:
            x2 = _rms(x2, D_MODEL) * fg_ref[...]
        o_ref[...] = x2


def _peer_call(h2_all, x1_all, modsel, wq, subk, u_tab, v_tab, fg, n_tok, n_lat, l_len, final):
    n_exp = u_tab.shape[0]
    n_keys = subk.shape[1]
    assert n_exp == n_keys * n_keys and n_keys % 16 == 0
    assert min(PEER_EB, n_exp) % (8 * n_keys) == 0
    assert PEER_SB % n_keys == 0 and min(PEER_EB, n_exp) % PEER_SB == 0
    assert n_tok % PEER_TT == 0 and n_lat % PEER_TT == 0 and l_len % PEER_TT == 0
    TT, EB = PEER_TT, min(PEER_EB, n_exp)
    lat_tiles = n_lat // TT
    per_b = l_len // TT

    def mod_map(i, e):
        return (jnp.where(i < lat_tiles, i // per_b, 0), (i < lat_tiles).astype(jnp.int32), 0, 0)

    n_blk = n_exp // EB
    vt_tab = jnp.transpose(v_tab.reshape(n_blk, EB, D_MODEL), (0, 2, 1))
    kern = functools.partial(_peer_kernel, n_keys=n_keys, final=final)
    full = lambda shape: pl.BlockSpec(shape, lambda i, e: (0,) * len(shape))
    return pl.pallas_call(
        kern,
        grid=(n_tok // TT, n_blk),
        in_specs=[pl.BlockSpec((TT, D_MODEL), lambda i, e: (i, 0)),
                  pl.BlockSpec((TT, D_MODEL), lambda i, e: (i, 0)),
                  pl.BlockSpec((1, 1, 6, D_MODEL), mod_map),
                  full(wq.shape), full(subk.shape),
                  pl.BlockSpec((EB, D_MODEL), lambda i, e: (e, 0)),
                  pl.BlockSpec((1, D_MODEL, EB), lambda i, e: (e, 0, 0)),
                  full((1, D_MODEL))],
        out_specs=pl.BlockSpec((TT, D_MODEL), lambda i, e: (i, 0)),
        out_shape=jax.ShapeDtypeStruct((n_tok, D_MODEL), F32),
        scratch_shapes=[pltpu.VMEM((D_MODEL, TT), BF16),
                        pltpu.VMEM((TT, PEER_HEADS * 256), BF16),
                        pltpu.VMEM((TT // LANES, PEER_HEADS, 2 * (n_keys + PEER_PAD), LANES), F32),
                        pltpu.VMEM((PEER_HEADS, n_keys, TT), F32),
                        pltpu.VMEM((PEER_HEADS, n_keys, TT), F32),
                        pltpu.VMEM((PEER_TOPK, TT), F32),
                        pltpu.VMEM((PEER_TOPK, TT), F32),
                        pltpu.VMEM((TT // LANES, EB, LANES), F32),
                        pltpu.VMEM((TT // LANES, EB, LANES), BF16),
                        pltpu.VMEM((D_MODEL, TT), F32)],
        compiler_params=_cparams(("arbitrary", "arbitrary")),
        name="peer_ffn",
    )(h2_all, x1_all, modsel, wq, subk, u_tab, vt_tab, fg)


def _head_slots(w, n_heads, width=HEAD_DIM):
    lead = w.shape[:-1]
    w = w.reshape(lead + (n_heads, width))
    w = jnp.pad(w, [(0, 0)] * len(lead) + [(0, 0), (0, HEAD_SLOT - width)])
    return w.reshape(lead + (n_heads * HEAD_SLOT,))


def _prep_w_in(w_in):
    sizes = (256, 128, 128, B_Q_LORA, B_KV_LORA, B_ROPE, 256, 256, 256, 256)
    offs = np.cumsum((0,) + sizes)
    aq, ak, av, bq, bkv, bkr, cu, dq, dk, dv = [w_in[:, offs[i]:offs[i + 1]] for i in range(10)]
    d = w_in.shape[0]
    bkr_p = jnp.concatenate([jnp.zeros((d, B_NOPE), F32), bkr, jnp.zeros((d, HEAD_SLOT - B_NOPE - B_ROPE), F32)], 1)
    cols = [_head_slots(aq, 4), _head_slots(ak, 2), _head_slots(av, 2),
            jnp.pad(bq, ((0, 0), (0, 256 - B_Q_LORA))), bkv, bkr_p, cu,
            _head_slots(dq, 4), _head_slots(dk, 4), _head_slots(dv, 4)]
    w = jnp.concatenate(cols, axis=1)
    assert w.shape[1] == _C_END
    return w.astype(BF16)


def _prep_mla(w_uq, w_ukv, qg, kvg):
    wq = _head_slots(w_uq, N_HEADS, B_NOPE + B_ROPE)
    wq = jnp.pad(wq, ((0, 256 - B_Q_LORA), (0, 0))).astype(BF16)
    kv = w_ukv.reshape(B_KV_LORA, N_HEADS, B_NOPE + HEAD_DIM)
    wk = _head_slots(kv[:, :, :B_NOPE].reshape(B_KV_LORA, -1), N_HEADS, B_NOPE)
    wv = _head_slots(kv[:, :, B_NOPE:].reshape(B_KV_LORA, -1), N_HEADS, HEAD_DIM)
    wkv = jnp.concatenate([wk, wv], axis=1).astype(BF16)
    qg_p = jnp.pad(qg, (0, 256 - B_Q_LORA)).reshape(1, 256)
    return wq, wkv, qg_p, kvg.reshape(1, B_KV_LORA)


def _rope_tables(ctx_len, l_len):
    t = np.arange(l_len)
    pos = np.stack([t // GRID_W, t % GRID_W], 0).astype(np.float64)

    def build(width, lane0):
        a = width // 2
        half = a // 2
        inv = ROPE_BASE ** (-np.arange(half, dtype=np.float64) / half)
        cos = np.ones((l_len, LANES)); sp = np.zeros((l_len, LANES)); sm = np.zeros((l_len, LANES))
        for j in range(width):
            axis, i = j // a, j % a
            f, second = i % half, i >= half
            ang = pos[axis] * inv[f]
            cos[:, lane0 + j] = np.cos(ang)
            if second:
                sp[:, lane0 + j] = np.sin(ang)
            else:
                sm[:, lane0 + j] = -np.sin(ang)
        tab = np.stack([cos, sp, sm], 0)
        ctx = np.stack([np.ones((ctx_len, LANES)), np.zeros((ctx_len, LANES)), np.zeros((ctx_len, LANES))], 0)
        return jnp.asarray(np.concatenate([ctx, tab], axis=1), F32)

    return build(HEAD_DIM, 0), build(B_ROPE, B_NOPE)


def _s5_matrices(lam_re, lam_im, log_step, b_re, b_im, c_re, c_im):
    lr, li = lam_re.astype(F32), lam_im.astype(F32)
    step = jnp.exp(log_step.astype(F32))[..., None]
    mag = jnp.exp(lr * step)
    abr, abi = mag * jnp.cos(li * step), mag * jnp.sin(li * step)
    nr, ni = abr - 1.0, abi
    den = lr * lr + li * li
    fr = (nr * lr + ni * li) / den
    fi = (ni * lr - nr * li) / den
    bbr = fr[..., None] * b_re - fi[..., None] * b_im
    bbi = fr[..., None] * b_im + fi[..., None] * b_re
    eye = jnp.eye(C_NGROUPS, dtype=F32)
    bm_r = jnp.einsum('dgpc,gh->dgchp', bbr, eye).reshape(2, GROUP_WIDTH, N_STATE)
    bm_i = jnp.einsum('dgpc,gh->dgchp', bbi, eye).reshape(2, GROUP_WIDTH, N_STATE)
    bmat = jnp.concatenate([bm_r, bm_i], axis=2).astype(BF16)
    cm_r = jnp.einsum('dgcp,gh->dgphc', c_re.astype(F32), eye).reshape(2, N_STATE, GROUP_WIDTH)
    cm_i = jnp.einsum('dgcp,gh->dgphc', c_im.astype(F32), eye).reshape(2, N_STATE, GROUP_WIDTH)
    cmat = jnp.concatenate([cm_r, -cm_i], axis=1).astype(BF16)
    return bmat, cmat, abr.reshape(2, 1, N_STATE), abi.reshape(2, 1, N_STATE)


def _na_bias(rpb):
    w = np.arange(GRID_W)
    cs = np.clip(w - NA_KW // 2, 0, GRID_W - NA_KW)
    c = np.arange(GRID_W)
    inwin = (c[None, :] >= cs[:, None]) & (c[None, :] < cs[:, None] + NA_KW)
    colidx = np.clip(c[None, :] - w[:, None] + NA_KW - 1, 0, 2 * NA_KW - 2)
    col_sel = (colidx[:, :, None] == np.arange(2 * NA_KW - 1)).astype(np.float32)
    d, y = np.arange(NA_KH)[:, None], np.arange(NA_KH)[None, :]
    row_sel = ((y - d + NA_KH - 1)[:, :, None] == np.arange(2 * NA_KH - 1)).astype(np.float32)
    hi = lax.Precision.HIGHEST
    t = jnp.einsum('hrq,wcq->hrwc', rpb.astype(F32), jnp.asarray(col_sel), precision=hi)
    b = jnp.einsum('dyr,hrwc->dhwyc', jnp.asarray(row_sel), t, precision=hi)
    b = jnp.where(jnp.asarray(inwin)[None, None, :, None, :], b, NEG_INF)
    b = b.reshape(NA_KH, N_HEADS, GRID_W, NA_KEYS)
    return jnp.concatenate([b, jnp.full((1, N_HEADS, GRID_W, NA_KEYS), NEG_INF, F32)], axis=0)


def _mix_layout(mix_norm_g, w_out):
    g = mix_norm_g.reshape(4, GROUP_WIDTH)
    w = w_out.reshape(4, GROUP_WIDTH, D_MODEL)
    gs, ws = [], []
    for k in range(4):
        if k == 2:
            gs.append(g[k]); ws.append(w[k])
        else:
            gs.append(_head_slots(g[k], N_HEADS))
            wk = w[k].reshape(N_HEADS, HEAD_DIM, D_MODEL)
            wk = jnp.pad(wk, ((0, 0), (0, HEAD_SLOT - HEAD_DIM), (0, 0))).reshape(N_HEADS * HEAD_SLOT, D_MODEL)
            ws.append(wk)
    return jnp.concatenate(gs).reshape(1, -1), jnp.concatenate(ws, axis=0).astype(BF16)


def kernel(x, c, ctx, c_ctx, norm1_g, norm2_g, w_ada, b_ada, w_in, swa_sink, mla_q_norm_g, mla_w_uq,
           mla_kv_norm_g, mla_w_ukv, s5_lambda_re, s5_lambda_im, s5_log_step, s5_b_re, s5_b_im, s5_c_re,
           s5_c_im, s5_d, s5_w_glu, na_rpb, mix_norm_g, w_out, peer_w_q, peer_sub_keys, peer_u, peer_v,
           final_norm_g):
    n_b, l_len, d = x.shape
    ctx_len = ctx.shape[1]
    depth = w_in.shape[0]
    s_len = ctx_len + l_len
    assert d == D_MODEL and ctx_len == TOKEN_TILE and n_b % 8 == 0 and n_b < 16
    assert l_len // GRID_W >= NA_KH and l_len % PEER_TT == 0
    n_lat = n_b * l_len

    cc = jnp.concatenate([c, c_ctx[None], jnp.zeros((16 - n_b - 1, d), F32)], axis=0)
    mod = _ada_call(cc, w_ada, b_ada).reshape(depth, 16, 6, d)
    lat_mod = mod[:, :n_b]
    ctx_mod = jnp.broadcast_to(mod[:, n_b:n_b + 1], lat_mod.shape)
    modsel = jnp.stack([ctx_mod, lat_mod], axis=2)

    tab_a, tab_b = _rope_tables(ctx_len, l_len)
    x_all = jnp.concatenate([x.reshape(n_lat, d), ctx.reshape(n_b * ctx_len, d)], axis=0)
    fg = final_norm_g.reshape(1, d)

    for l in range(depth):
        last = l == depth - 1
        w_in_p = _prep_w_in(w_in[l])
        wuq, wukv, qg, kvg = _prep_mla(mla_w_uq[l], mla_w_ukv[l], mla_q_norm_g[l], mla_kv_norm_g[l])
        qa, ka, va, qb, kb, vb, cu, qd, kd, vd = _proj_call(
            x_all, modsel[l], norm1_g[l].reshape(1, d), w_in_p, tab_a, tab_b, qg, wuq, kvg, wukv, n_b, s_len)

        ya = _swa_call(swa_sink[l], qa, ka, va, ctx_len)
        yb = _mla_call(qb, kb, vb, ctx_len)

        bmat, cmat, a_r, a_i = _s5_matrices(s5_lambda_re[l], s5_lambda_im[l], s5_log_step[l],
                                            s5_b_re[l], s5_b_im[l], s5_c_re[l], s5_c_im[l])
        cu_t = cu.reshape(s_len, n_b, GROUP_WIDTH)
        y_dirs = _s5_call(cu_t, bmat, cmat, a_r, a_i, ctx_len)
        ys_t = _s5_glu_call(cu.reshape(s_len * n_b, GROUP_WIDTH), y_dirs.reshape(2, s_len * n_b, GROUP_WIDTH),
                            s5_d[l].reshape(1, GROUP_WIDTH), s5_w_glu[l].astype(BF16))
        ys = ys_t.reshape(s_len, n_b * GROUP_WIDTH)

        yd = _na_call(qd, kd, vd, _na_bias(na_rpb[l]), ctx_len)

        mixg_p, w_out_p = _mix_layout(mix_norm_g[l], w_out[l])
        x1_all, h2_all = _out_call(ya, yb, ys, yd, x_all, modsel[l], mixg_p, w_out_p, norm2_g[l].reshape(1, d))

        n_tok = n_lat if last else x_all.shape[0]
        x_all = _peer_call(h2_all, x1_all, modsel[l], peer_w_q[l].astype(BF16), peer_sub_keys[l].astype(BF16),
                           peer_u[l].astype(BF16), peer_v[l].astype(BF16), fg, n_tok, n_lat, l_len, last)

    return x_all[:n_lat].reshape(n_b, l_len, d)
```

```python
import functools
import math

import numpy as np
import jax
import jax.numpy as jnp
from jax import lax
from jax.experimental import pallas as pl
from jax.experimental.pallas import tpu as pltpu

F32 = jnp.float32
BF16 = jnp.bfloat16

D_MODEL = 1024
HEAD_DIM = 64
GROUP_WIDTH = 256
GRID_W = 64
EPS = 1e-6
NEG_INF = -1e30
ROPE_BASE = 10000.0
N_HEADS = 4
A_KV_HEADS = 2
A_WINDOW = 128
A_BLOCK = 128
B_NOPE = 64
B_ROPE = 32
B_Q_LORA = 192
B_KV_LORA = 128
C_GROUP = 16
C_NGROUPS = 16
C_STATE = 64
NA_KH = 8
NA_KW = 16
PEER_HEADS = 8
PEER_TOPK = 16

LANES = 128
HEAD_SLOT = LANES
TOKEN_TILE = 256
VMEM_LIMIT = 60 * 1024 * 1024

N_STATE = C_NGROUPS * C_STATE


def _cparams(sem):
    return pltpu.CompilerParams(dimension_semantics=sem, vmem_limit_bytes=VMEM_LIMIT)


def _nt_dot(a, b):
    return lax.dot_general(a, b, (((1,), (1,)), ((), ())), preferred_element_type=F32)


def _dot(a, b):
    return jnp.dot(a, b, preferred_element_type=F32)


def _ada_kernel(c_ref, w_ref, b_ref, o_ref):
    c = c_ref[...]
    s = c * jax.nn.sigmoid(c)
    o_ref[0] = _dot(s.astype(BF16), w_ref[0].astype(BF16)) + b_ref[0]


def _ada_call(cc, w_ada, b_ada):
    depth, d, n6 = w_ada.shape
    tn = 1536
    return pl.pallas_call(
        _ada_kernel,
        grid=(depth, n6 // tn),
        in_specs=[pl.BlockSpec((cc.shape[0], d), lambda l, j: (0, 0)),
                  pl.BlockSpec((1, d, tn), lambda l, j: (l, 0, j)),
                  pl.BlockSpec((1, 1, tn), lambda l, j: (l, 0, j))],
        out_specs=pl.BlockSpec((1, cc.shape[0], tn), lambda l, j: (l, 0, j)),
        out_shape=jax.ShapeDtypeStruct((depth, cc.shape[0], n6), F32),
        compiler_params=_cparams(("arbitrary", "arbitrary")),
        name="ada_mod",
    )(cc, w_ada, b_ada.reshape(depth, 1, n6))


_C_AQ = 0
_C_AK = _C_AQ + 4 * HEAD_SLOT
_C_AV = _C_AK + 2 * HEAD_SLOT
_C_BQ = _C_AV + 2 * HEAD_SLOT
_C_BKV = _C_BQ + 256
_C_BKR = _C_BKV + 128
_C_CU = _C_BKR + 128
_C_DQ = _C_CU + 256
_C_DK = _C_DQ + 4 * HEAD_SLOT
_C_DV = _C_DK + 4 * HEAD_SLOT
_C_END = _C_DV + 4 * HEAD_SLOT


def _rms(x, n):
    return x * lax.rsqrt(jnp.sum(x * x, axis=-1, keepdims=True) * (1.0 / n) + EPS)


def _rope(t, tab_ref, shift):
    return (t * tab_ref[0]
            + pltpu.roll(t, shift, 1) * tab_ref[1]
            + pltpu.roll(t, LANES - shift, 1) * tab_ref[2])


def _proj_kernel(x_ref, mod_ref, g_ref, w_ref, ta_ref, tb_ref, qg_ref, wuq_ref, kvg_ref, wukv_ref,
                 qa_ref, ka_ref, va_ref, qb_ref, kb_ref, vb_ref, cu_ref, qd_ref, kd_ref, vd_ref):
    x = x_ref[...]
    m = mod_ref[0, 0]
    h = _rms(x, D_MODEL) * g_ref[...]
    h = h * (1.0 + m[1:2]) + m[0:1]
    z = _dot(h.astype(BF16), w_ref[...])

    a_scale = HEAD_DIM ** -0.5
    for hh in range(N_HEADS):
        c0 = _C_AQ + hh * HEAD_SLOT
        qa_ref[0, :, hh * HEAD_SLOT:(hh + 1) * HEAD_SLOT] = (
            _rope(z[:, c0:c0 + HEAD_SLOT], ta_ref, 16) * a_scale).astype(BF16)
    for g in range(A_KV_HEADS):
        c0 = _C_AK + g * HEAD_SLOT
        ka_ref[0, :, g * HEAD_SLOT:(g + 1) * HEAD_SLOT] = _rope(z[:, c0:c0 + HEAD_SLOT], ta_ref, 16).astype(BF16)
    va_ref[0] = z[:, _C_AV:_C_AV + 2 * HEAD_SLOT].astype(BF16)

    cq = _rms(z[:, _C_BQ:_C_BQ + 256], B_Q_LORA) * qg_ref[...]
    qb = _dot(cq.astype(BF16), wuq_ref[...])
    ckv = _rms(z[:, _C_BKV:_C_BKV + 128], B_KV_LORA) * kvg_ref[...]
    kv = _dot(ckv.astype(BF16), wukv_ref[...])
    kr = _rope(z[:, _C_BKR:_C_BKR + 128], tb_ref, 8)
    lane = lax.broadcasted_iota(jnp.int32, (1, HEAD_SLOT), 1)
    ones_col = (lane == HEAD_DIM).astype(F32)
    for hh in range(N_HEADS):
        sl = slice(hh * HEAD_SLOT, (hh + 1) * HEAD_SLOT)
        qb_ref[0, :, sl] = _rope(qb[:, sl], tb_ref, 8).astype(BF16)
        kb_ref[0, :, sl] = (kv[:, sl] + kr).astype(BF16)
        vsl = slice(4 * HEAD_SLOT + hh * HEAD_SLOT, 4 * HEAD_SLOT + (hh + 1) * HEAD_SLOT)
        vb_ref[0, :, sl] = (kv[:, vsl] + ones_col).astype(BF16)

    cu_ref[...] = z[:, _C_CU:_C_CU + 256]
    qd_ref[0] = (z[:, _C_DQ:_C_DQ + 4 * HEAD_SLOT] * a_scale).astype(BF16)
    kd_ref[0] = z[:, _C_DK:_C_DK + 4 * HEAD_SLOT].astype(BF16)
    vd_ref[0] = z[:, _C_DV:_C_DV + 4 * HEAD_SLOT].astype(BF16)


def _stream_block(b, t, n_b, lat_tiles):
    return jnp.where(t == 0, n_b * lat_tiles + b, b * lat_tiles + t - 1)


def _proj_call(x_all, modsel, g1, w_in_p, tab_a, tab_b, qg, wuq, kvg, wukv, n_b, s_len):
    tiles = s_len // TOKEN_TILE
    lat_tiles = tiles - 1
    T = TOKEN_TILE
    xmap = lambda b, t: (_stream_block(b, t, n_b, lat_tiles), 0)
    full = lambda shape: pl.BlockSpec(shape, lambda b, t: (0,) * len(shape))
    omap = lambda b, t: (b, t, 0)
    widths = [4 * HEAD_SLOT, 2 * HEAD_SLOT, 2 * HEAD_SLOT, 4 * HEAD_SLOT, 4 * HEAD_SLOT, 4 * HEAD_SLOT,
              256, 4 * HEAD_SLOT, 4 * HEAD_SLOT, 4 * HEAD_SLOT]
    dts = [BF16, BF16, BF16, BF16, BF16, BF16, F32, BF16, BF16, BF16]
    return pl.pallas_call(
        _proj_kernel,
        grid=(n_b, tiles),
        in_specs=[pl.BlockSpec((T, D_MODEL), xmap),
                  pl.BlockSpec((1, 1, 6, D_MODEL), lambda b, t: (b, jnp.minimum(t, 1), 0, 0)),
                  full((1, D_MODEL)),
                  full(w_in_p.shape),
                  pl.BlockSpec((3, T, LANES), lambda b, t: (0, t, 0)),
                  pl.BlockSpec((3, T, LANES), lambda b, t: (0, t, 0)),
                  full(qg.shape), full(wuq.shape), full(kvg.shape), full(wukv.shape)],
        out_specs=[pl.BlockSpec((T, w), lambda b, t: (t, b)) if i == 6 else pl.BlockSpec((1, T, w), omap)
                   for i, w in enumerate(widths)],
        out_shape=[jax.ShapeDtypeStruct((s_len, n_b * w) if i == 6 else (n_b, s_len, w), dt)
                   for i, (w, dt) in enumerate(zip(widths, dts))],
        compiler_params=_cparams(("arbitrary", "arbitrary")),
        name="mod_proj",
    )(x_all, modsel, g1, w_in_p, tab_a, tab_b, qg, wuq, kvg, wukv)


def _swa_kernel(sink_ref, q_ref, k_ref, v_ref, o_ref, *, ctx_len, s_len):
    n = pl.program_id(1)
    is_lat = n >= ctx_len // A_BLOCK
    band = 3 * A_BLOCK
    ks = pl.multiple_of(jnp.clip((n - 1) * A_BLOCK, 0, s_len - band), A_BLOCK)
    q = q_ref[0]
    kl = k_ref[0, pl.ds(ks, band), :]
    vl = v_ref[0, pl.ds(ks, band), :]
    kc = k_ref[0, 0:ctx_len, :]
    vc = v_ref[0, 0:ctx_len, :]
    qpos = n * A_BLOCK + lax.broadcasted_iota(jnp.int32, (A_BLOCK, band), 0)
    kpos = ks + lax.broadcasted_iota(jnp.int32, (A_BLOCK, band), 1)
    window = jnp.where(is_lat, A_WINDOW, -1)
    valid = (jnp.abs(qpos - kpos) <= window) & (kpos >= ctx_len)
    rep = N_HEADS // A_KV_HEADS
    scores = []
    for hh in range(N_HEADS):
        qs = slice(hh * HEAD_SLOT, (hh + 1) * HEAD_SLOT)
        gs = slice((hh // rep) * HEAD_SLOT, (hh // rep + 1) * HEAD_SLOT)
        scores.append((_nt_dot(q[:, qs], kl[:, gs]), _nt_dot(q[:, qs], kc[:, gs])))
    for hh in range(N_HEADS):
        g = hh // rep
        qs = slice(hh * HEAD_SLOT, (hh + 1) * HEAD_SLOT)
        gs = slice(g * HEAD_SLOT, (g + 1) * HEAD_SLOT)
        s_loc = jnp.where(valid, scores[hh][0], NEG_INF)
        s_ctx = scores[hh][1]
        sk = sink_ref[hh]
        mx = jnp.maximum(jnp.max(s_loc, axis=-1, keepdims=True), jnp.max(s_ctx, axis=-1, keepdims=True))
        mx = jnp.maximum(mx, sk)
        p_loc = jnp.exp(s_loc - mx)
        p_ctx = jnp.exp(s_ctx - mx)
        den = (jnp.sum(p_loc, axis=-1, keepdims=True) + jnp.sum(p_ctx, axis=-1, keepdims=True)
               + jnp.exp(sk - mx))
        o = _dot(p_loc.astype(BF16), vl[:, gs]) + _dot(p_ctx.astype(BF16), vc[:, gs])
        o_ref[0, :, qs] = o * (1.0 / den)


def _swa_call(sink, qa, ka, va, ctx_len):
    n_b, s_len, _ = qa.shape
    kern = functools.partial(_swa_kernel, ctx_len=ctx_len, s_len=s_len)
    return pl.pallas_call(
        kern,
        grid=(n_b, s_len // A_BLOCK),
        in_specs=[pl.BlockSpec(memory_space=pltpu.SMEM),
                  pl.BlockSpec((1, A_BLOCK, 4 * HEAD_SLOT), lambda b, n: (b, n, 0)),
                  pl.BlockSpec((1, s_len, 2 * HEAD_SLOT), lambda b, n: (b, 0, 0)),
                  pl.BlockSpec((1, s_len, 2 * HEAD_SLOT), lambda b, n: (b, 0, 0))],
        out_specs=pl.BlockSpec((1, A_BLOCK, 4 * HEAD_SLOT), lambda b, n: (b, n, 0)),
        out_shape=jax.ShapeDtypeStruct((n_b, s_len, 4 * HEAD_SLOT), F32),
        compiler_params=_cparams(("arbitrary", "arbitrary")),
        name="swa_mixer",
    )(sink, qa, ka, va)


MLA_TQ = 256


def _mla_kernel(q_ref, k_ref, v_ref, o_ref, *, ctx_len, s_len):
    t = pl.program_id(1)
    is_lat = t >= ctx_len // MLA_TQ
    scale = (B_NOPE + B_ROPE) ** -0.5
    lane = lax.broadcasted_iota(jnp.int32, (1, HEAD_SLOT), 1)

    def attend(n_keys):
        def scores(hh):
            sl = slice(hh * HEAD_SLOT, (hh + 1) * HEAD_SLOT)
            return _nt_dot(q_ref[0, :, sl], k_ref[0, 0:n_keys, sl]) * scale

        s_next = scores(0)
        for hh in range(N_HEADS):
            sl = slice(hh * HEAD_SLOT, (hh + 1) * HEAD_SLOT)
            s = s_next
            if hh + 1 < N_HEADS:
                s_next = scores(hh + 1)
            p = jnp.exp(s - jnp.max(s, axis=-1, keepdims=True))
            acc = _dot(p.astype(BF16), v_ref[0, 0:n_keys, sl])
            den = acc[:, HEAD_DIM:HEAD_DIM + 1]
            o_ref[0, :, sl] = jnp.where(lane < HEAD_DIM, acc * (1.0 / den), 0.0)

    @pl.when(is_lat)
    def _():
        attend(s_len)

    @pl.when(jnp.logical_not(is_lat))
    def _():
        attend(ctx_len)


def _mla_call(qb, kb, vb, ctx_len):
    n_b, s_len, _ = qb.shape
    assert ctx_len % MLA_TQ == 0 and s_len % MLA_TQ == 0
    kern = functools.partial(_mla_kernel, ctx_len=ctx_len, s_len=s_len)
    return pl.pallas_call(
        kern,
        grid=(n_b, s_len // MLA_TQ),
        in_specs=[pl.BlockSpec((1, MLA_TQ, 4 * HEAD_SLOT), lambda b, t: (b, t, 0)),
                  pl.BlockSpec((1, s_len, 4 * HEAD_SLOT), lambda b, t: (b, 0, 0)),
                  pl.BlockSpec((1, s_len, 4 * HEAD_SLOT), lambda b, t: (b, 0, 0))],
        out_specs=pl.BlockSpec((1, MLA_TQ, 4 * HEAD_SLOT), lambda b, t: (b, t, 0)),
        out_shape=jax.ShapeDtypeStruct((n_b, s_len, 4 * HEAD_SLOT), F32),
        compiler_params=_cparams(("arbitrary", "arbitrary")),
        name="mla_mixer",
    )(qb, kb, vb)


S5_T = 128


def _s5_kernel(u_ref, bm_ref, cm_ref, ar_ref, ai_ref, y_ref, hr_scr, hi_scr, bu_scr, *, n_b):
    d = pl.program_id(0)
    c = pl.program_id(1)

    @pl.when(c == 0)
    def _():
        hr_scr[...] = jnp.zeros_like(hr_scr)
        hi_scr[...] = jnp.zeros_like(hi_scr)

    u = u_ref[...].reshape(S5_T * n_b, GROUP_WIDTH)
    bu_scr[...] = _dot(u.astype(BF16), bm_ref[0])
    a_r = jnp.broadcast_to(ar_ref[0], (n_b, N_STATE))
    a_i = jnp.broadcast_to(ai_ref[0], (n_b, N_STATE))

    def step(i, carry):
        h_r, h_i = carry
        t = jnp.where(d == 0, i, S5_T - 1 - i)
        row = pl.multiple_of(t * n_b, n_b)
        b_r = bu_scr[pl.ds(row, n_b), 0:N_STATE]
        b_i = bu_scr[pl.ds(row, n_b), N_STATE:2 * N_STATE]
        n_r = a_r * h_r - a_i * h_i + b_r
        n_i = a_r * h_i + a_i * h_r + b_i
        bu_scr[pl.ds(row, n_b), 0:N_STATE] = n_r
        bu_scr[pl.ds(row, n_b), N_STATE:2 * N_STATE] = n_i
        return n_r, n_i

    h_r, h_i = lax.fori_loop(0, S5_T, step, (hr_scr[...], hi_scr[...]), unroll=4)
    hr_scr[...] = h_r
    hi_scr[...] = h_i
    y = _dot(bu_scr[...].astype(BF16), cm_ref[0])
    y_ref[0] = y.reshape(S5_T, n_b, GROUP_WIDTH)


def _s5_chunk(d, c, n_chunks, ctx_chunks):
    rev = jnp.where(c < ctx_chunks, ctx_chunks - 1 - c, n_chunks - 1 - (c - ctx_chunks))
    return jnp.where(d == 0, c, rev)


def _s5_call(cu_t, bmat, cmat, a_r, a_i, ctx_len):
    s_len, n_b, _ = cu_t.shape
    n_chunks = s_len // S5_T
    ctx_chunks = ctx_len // S5_T
    cmap = lambda d, c: (_s5_chunk(d, c, n_chunks, ctx_chunks), 0, 0)
    kern = functools.partial(_s5_kernel, n_b=n_b)
    return pl.pallas_call(
        kern,
        grid=(2, n_chunks),
        in_specs=[pl.BlockSpec((S5_T, n_b, GROUP_WIDTH), cmap),
                  pl.BlockSpec((1, GROUP_WIDTH, 2 * N_STATE), lambda d, c: (d, 0, 0)),
                  pl.BlockSpec((1, 2 * N_STATE, GROUP_WIDTH), lambda d, c: (d, 0, 0)),
                  pl.BlockSpec((1, 1, N_STATE), lambda d, c: (d, 0, 0)),
                  pl.BlockSpec((1, 1, N_STATE), lambda d, c: (d, 0, 0))],
        out_specs=pl.BlockSpec((1, S5_T, n_b, GROUP_WIDTH),
                               lambda d, c: (d, _s5_chunk(d, c, n_chunks, ctx_chunks), 0, 0)),
        out_shape=jax.ShapeDtypeStruct((2, s_len, n_b, GROUP_WIDTH), F32),
        scratch_shapes=[pltpu.VMEM((n_b, N_STATE), F32), pltpu.VMEM((n_b, N_STATE), F32),
                        pltpu.VMEM((S5_T * n_b, 2 * N_STATE), F32)],
        compiler_params=_cparams(("arbitrary", "arbitrary")),
        name="s5_scan",
    )(cu_t, bmat, cmat, a_r, a_i)


def _s5_glu_kernel(u_ref, y_ref, d_ref, w_ref, o_ref):
    y = u_ref[...] * d_ref[...] + y_ref[0] + y_ref[1]
    z = _dot(jax.nn.gelu(y).astype(BF16), w_ref[...])
    o_ref[...] = z[:, :GROUP_WIDTH] * jax.nn.sigmoid(z[:, GROUP_WIDTH:])


def _s5_glu_call(cu_flat, y_dirs, d_skip, w_glu):
    n = cu_flat.shape[0]
    tr = 1024
    return pl.pallas_call(
        _s5_glu_kernel,
        grid=(n // tr,),
        in_specs=[pl.BlockSpec((tr, GROUP_WIDTH), lambda i: (i, 0)),
                  pl.BlockSpec((2, tr, GROUP_WIDTH), lambda i: (0, i, 0)),
                  pl.BlockSpec((1, GROUP_WIDTH), lambda i: (0, 0)),
                  pl.BlockSpec((GROUP_WIDTH, 2 * GROUP_WIDTH), lambda i: (0, 0))],
        out_specs=pl.BlockSpec((tr, GROUP_WIDTH), lambda i: (i, 0)),
        out_shape=jax.ShapeDtypeStruct((n, GROUP_WIDTH), F32),
        compiler_params=_cparams(("arbitrary",)),
        name="s5_glu",
    )(cu_flat, y_dirs, d_skip, w_glu)


NA_KEYS = NA_KH * GRID_W


NA_RB = 4


def _na_kernel(q_ref, k_ref, v_ref, bias_ref, o_ref, *, ctx_len, n_rows):
    j = pl.program_id(1)
    ctx_steps = ctx_len // (NA_RB * GRID_W)
    kc = k_ref[0, 0:ctx_len, :]
    vc = v_ref[0, 0:ctx_len, :]
    scores = []
    for i in range(NA_RB):
        r = (j - ctx_steps) * NA_RB + i
        r0 = jnp.clip(r - NA_KH // 2, 0, n_rows - NA_KH)
        var = jnp.where(j < ctx_steps, NA_KH, r - r0)
        ks = pl.multiple_of(ctx_len + r0 * GRID_W, GRID_W)
        q = q_ref[0, i * GRID_W:(i + 1) * GRID_W, :]
        kl = k_ref[0, pl.ds(ks, NA_KEYS), :]
        for hh in range(N_HEADS):
            sl = slice(hh * HEAD_SLOT, (hh + 1) * HEAD_SLOT)
            scores.append((_nt_dot(q[:, sl], kl[:, sl]) + bias_ref[var, hh], _nt_dot(q[:, sl], kc[:, sl]), ks))
    for i in range(NA_RB):
        qs = slice(i * GRID_W, (i + 1) * GRID_W)
        for hh in range(N_HEADS):
            sl = slice(hh * HEAD_SLOT, (hh + 1) * HEAD_SLOT)
            s_loc, s_ctx, ks = scores[i * N_HEADS + hh]
            mx = jnp.maximum(jnp.max(s_loc, axis=-1, keepdims=True), jnp.max(s_ctx, axis=-1, keepdims=True))
            p_loc = jnp.exp(s_loc - mx)
            p_ctx = jnp.exp(s_ctx - mx)
            den = jnp.sum(p_loc, axis=-1, keepdims=True) + jnp.sum(p_ctx, axis=-1, keepdims=True)
            o = (_dot(p_loc.astype(BF16), v_ref[0, pl.ds(ks, NA_KEYS), sl])
                 + _dot(p_ctx.astype(BF16), vc[:, sl]))
            o_ref[0, qs, sl] = o * (1.0 / den)


def _na_call(qd, kd, vd, bias, ctx_len):
    n_b, s_len, _ = qd.shape
    n_rows = (s_len - ctx_len) // GRID_W
    blk = NA_RB * GRID_W
    assert ctx_len % blk == 0 and n_rows % NA_RB == 0
    kern = functools.partial(_na_kernel, ctx_len=ctx_len, n_rows=n_rows)
    return pl.pallas_call(
        kern,
        grid=(n_b, s_len // blk),
        in_specs=[pl.BlockSpec((1, blk, 4 * HEAD_SLOT), lambda b, j: (b, j, 0)),
                  pl.BlockSpec((1, s_len, 4 * HEAD_SLOT), lambda b, j: (b, 0, 0)),
                  pl.BlockSpec((1, s_len, 4 * HEAD_SLOT), lambda b, j: (b, 0, 0)),
                  pl.BlockSpec(bias.shape, lambda b, j: (0, 0, 0, 0))],
        out_specs=pl.BlockSpec((1, blk, 4 * HEAD_SLOT), lambda b, j: (b, j, 0)),
        out_shape=jax.ShapeDtypeStruct((n_b, s_len, 4 * HEAD_SLOT), F32),
        compiler_params=_cparams(("arbitrary", "arbitrary")),
        name="na_mixer",
    )(qd, kd, vd, bias)


def _out_kernel(ya_ref, yb_ref, cu_ref, ydir_ref, sd_ref, wg_ref, yd_ref, x_ref, mod_ref, mg_ref, w_ref, g2_ref,
                x1_ref, h2_ref):
    m = mod_ref[0, 0]
    mg = mg_ref[...]
    s5 = cu_ref[...] * sd_ref[...] + ydir_ref[0] + ydir_ref[1]
    zg = _dot(jax.nn.gelu(s5).astype(BF16), wg_ref[...])
    ys = zg[:, :GROUP_WIDTH] * jax.nn.sigmoid(zg[:, GROUP_WIDTH:])
    parts = []
    off = 0
    for src, width in ((ya_ref, 4 * HEAD_SLOT), (yb_ref, 4 * HEAD_SLOT), (ys, GROUP_WIDTH),
                       (yd_ref, 4 * HEAD_SLOT)):
        y_k = src if src is ys else src[0]
        parts.append((_rms(y_k, GROUP_WIDTH) * mg[:, off:off + width]).astype(BF16))
        off += width
    y = _dot(jnp.concatenate(parts, axis=-1), w_ref[...])
    x1 = x_ref[...] + m[2:3] * y
    x1_ref[...] = x1
    h2 = _rms(x1, D_MODEL) * g2_ref[...]
    h2_ref[...] = (h2 * (1.0 + m[4:5]) + m[3:4]).astype(BF16)


def _out_call(ya, yb, cu, y_dirs, d_skip, w_glu, yd, x_all, modsel, mixg_p, w_out_p, g2):
    n_b, s_len, _ = ya.shape
    tiles = s_len // TOKEN_TILE
    lat_tiles = tiles - 1
    T = TOKEN_TILE
    xmap = lambda b, t: (_stream_block(b, t, n_b, lat_tiles), 0)
    full = lambda shape: pl.BlockSpec(shape, lambda b, t: (0,) * len(shape))
    ymap = lambda b, t: (b, t, 0)
    return pl.pallas_call(
        _out_kernel,
        grid=(n_b, tiles),
        in_specs=[pl.BlockSpec((1, T, ya.shape[-1]), ymap), pl.BlockSpec((1, T, yb.shape[-1]), ymap),
                  pl.BlockSpec((T, GROUP_WIDTH), lambda b, t: (t, b)),
                  pl.BlockSpec((2, T, GROUP_WIDTH), lambda b, t: (0, t, b)),
                  full(d_skip.shape), full(w_glu.shape),
                  pl.BlockSpec((1, T, yd.shape[-1]), ymap),
                  pl.BlockSpec((T, D_MODEL), xmap),
                  pl.BlockSpec((1, 1, 6, D_MODEL), lambda b, t: (b, jnp.minimum(t, 1), 0, 0)),
                  full(mixg_p.shape), full(w_out_p.shape), full((1, D_MODEL))],
        out_specs=[pl.BlockSpec((T, D_MODEL), xmap), pl.BlockSpec((T, D_MODEL), xmap)],
        out_shape=[jax.ShapeDtypeStruct(x_all.shape, F32), jax.ShapeDtypeStruct(x_all.shape, BF16)],
        compiler_params=_cparams(("arbitrary", "arbitrary")),
        name="out_proj",
    )(ya, yb, cu, y_dirs, d_skip, w_glu, yd, x_all, modsel, mixg_p, w_out_p, g2)


PEER_TT = 512
PEER_EB = 2048
PEER_SB = 512
PEER_CW = 512
PEER_IGRP = 2
PEER_PAD = 8


_GELU_B = -2.0 * math.sqrt(2.0 / math.pi) * math.log2(math.e)
_GELU_A = 0.044715 * _GELU_B


def _gelu_tanh(x):
    e = jnp.exp2(x * (x * x * _GELU_A + _GELU_B))
    return x * (1.0 / (1.0 + e))


def _peer_kernel(h2_ref, x1_ref, mod_ref, wq_ref, sk_ref, u_ref, vt_ref, fg_ref, o_ref,
                 h2t_scr, q_scr, g_scr, n_scr, c1_scr, v1_scr, v2_scr, z_scr, p_scr, out_scr,
                 *, n_keys, final):
    e = pl.program_id(1)
    n_e = pl.num_programs(1)
    TT = PEER_TT
    LT = TT // LANES
    NK = n_keys
    e2_row0 = NK + PEER_PAD
    neg = -jnp.inf

    @pl.when(e == 0)
    def _():
        h2 = h2_ref[...]
        h2t_scr[...] = h2.astype(F32).T.astype(BF16)
        q_scr[...] = _dot(h2, wq_ref[...]).astype(BF16)
        out_scr[...] = jnp.zeros_like(out_scr)

        def extract(s, v_scr, ls, want_rank):
            w = s
            rank = jnp.full(s.shape, 127.0, F32) if want_rank else None
            for k in range(PEER_TOPK):
                mk = jnp.max(w, axis=0, keepdims=True)
                v_scr[k:k + 1, ls] = mk
                hit = w == mk
                if want_rank:
                    rank = jnp.where(hit, float(k), rank)
                if k + 1 < PEER_TOPK:
                    w = jnp.where(hit, neg, w)
            return rank

        def head(hh, carry):
            c0 = pl.multiple_of(hh * 256, 256)
            s1_all = _nt_dot(sk_ref[0], q_scr[:, pl.ds(c0, 128)])
            s2_all = _nt_dot(sk_ref[1], q_scr[:, pl.ds(c0 + 128, 128)])
            for lt in range(LT):
                ls = slice(lt * LANES, (lt + 1) * LANES)
                s1 = s1_all[:, ls]
                s2 = s2_all[:, ls]
                extract(s1, v1_scr, ls, False)
                r2 = extract(s2, v2_scr, ls, True)
                slabs = [v1_scr[0:1, ls] + v2_scr[0:16, ls]]
                for a in range(1, 5):
                    slabs.append(v1_scr[a:a + 1, ls] + v2_scr[0:8, ls])
                slabs.append(v1_scr[0:8, ls] + v2_scr[1:2, ls])
                slabs.append(v1_scr[0:8, ls] + v2_scr[0:1, ls])
                slabs.append(v1_scr[8:16, ls] + v2_scr[0:1, ls])
                cand = jnp.concatenate(slabs, axis=0)
                top = v1_scr[0:1, ls] + v2_scr[0:1, ls]
                zsum = jnp.zeros((1, LANES), F32)
                tau = top
                for k in range(PEER_TOPK):
                    tau = jnp.max(cand, axis=0, keepdims=True)
                    zsum = zsum + jnp.exp(tau - top)
                    if k + 1 < PEER_TOPK:
                        cand = jnp.where(cand == tau, neg, cand)
                count = jnp.zeros(s1.shape, F32)
                for a in range(PEER_TOPK):
                    v1a = v1_scr[a:a + 1, ls]
                    sel = (v1a + v2_scr[0:16, ls]) >= tau
                    cnt = jnp.sum(jnp.where(sel, 1.0, 0.0), axis=0, keepdims=True)
                    count = jnp.where(s1 == v1a, cnt, count)
                n_scr[hh, :, ls] = count
                c1_scr[hh, :, ls] = jnp.exp(s1 - v1_scr[0:1, ls]) * (1.0 / zsum)
                g_scr[lt, hh, 0:NK, :] = r2
                g_scr[lt, hh, e2_row0:e2_row0 + NK, :] = jnp.exp(s2 - v2_scr[0:1, ls])
            return carry

        lax.fori_loop(0, PEER_HEADS, head, 0)

    EB = u_ref.shape[0]
    i_per = EB // NK
    i_per_sb = PEER_SB // NK
    jt_n = NK // 16
    lt_per = PEER_CW // LANES
    n_half = TT // PEER_CW
    chunks = [(sb, hf) for hf in range(n_half) for sb in range(EB // PEER_SB)]
    i_grp = min(PEER_IGRP, i_per_sb)

    def stage_z(c):
        sb, hf = chunks[c]
        rows = slice(sb * PEER_SB, (sb + 1) * PEER_SB)
        z = _dot(u_ref[rows, :], h2t_scr[:, hf * PEER_CW:(hf + 1) * PEER_CW])
        for l in range(lt_per):
            z_scr[hf * lt_per + l, rows, :] = z[:, l * LANES:(l + 1) * LANES]

    def stage_gate(c):
        sb, hf = chunks[c]
        for lt in range(hf * lt_per, (hf + 1) * lt_per):
            ls = slice(lt * LANES, (lt + 1) * LANES)
            for g0 in range(0, i_per_sb, i_grp):
                nrow, crow = [], []
                for i2 in range(g0, g0 + i_grp):
                    ii = sb * i_per_sb + i2
                    i8 = pl.multiple_of(e * i_per + (ii // 8) * 8, 8)
                    r8 = slice(ii % 8, ii % 8 + 1)
                    nrow.append([jnp.broadcast_to(n_scr[hh, pl.ds(i8, 8), ls][r8], (8, LANES))
                                 for hh in range(PEER_HEADS)])
                    crow.append([jnp.broadcast_to(c1_scr[hh, pl.ds(i8, 8), ls][r8], (8, LANES))
                                 for hh in range(PEER_HEADS)])
                for jt in range(jt_n):
                    halves = [[None, None] for _ in range(i_grp)]
                    for s8 in range(2):
                        j0 = jt * 16 + s8 * 8
                        w = [None] * i_grp
                        for hh in range(PEER_HEADS):
                            r2t = g_scr[lt, hh, j0:j0 + 8, :]
                            e2t = g_scr[lt, hh, e2_row0 + j0:e2_row0 + j0 + 8, :]
                            for k in range(i_grp):
                                term = jnp.where(r2t < nrow[k][hh], e2t, 0.0) * crow[k][hh]
                                w[k] = term if w[k] is None else w[k] + term
                        for k in range(i_grp):
                            r0 = sb * PEER_SB + (g0 + k) * NK + j0
                            halves[k][s8] = w[k] * _gelu_tanh(z_scr[lt, r0:r0 + 8, :])
                    for k in range(i_grp):
                        r0 = sb * PEER_SB + (g0 + k) * NK + jt * 16
                        p_scr[lt, r0:r0 + 16, :] = jnp.concatenate(halves[k], axis=0).astype(BF16)

    def stage_out(hf):
        p_h = jnp.concatenate([p_scr[hf * lt_per + l] for l in range(lt_per)], axis=1)
        cs = slice(hf * PEER_CW, (hf + 1) * PEER_CW)
        out_scr[:, cs] += _dot(vt_ref[0], p_h)

    for c in range(len(chunks)):
        stage_z(c)
    for c in range(len(chunks)):
        stage_gate(c)
        if (c + 1) % (EB // PEER_SB) == 0:
            stage_out(chunks[c][1])

    @pl.when(e == n_e - 1)
    def _():
        x2 = x1_ref[...] + mod_ref[0, 0][5:6] * out_scr[...].T
        if final:
            x2 = _rms(x2, D_MODEL) * fg_ref[...]
        o_ref[...] = x2


def _peer_call(h2_all, x1_all, modsel, wq, subk, u_tab, v_tab, fg, n_tok, n_lat, l_len, final):
    n_exp = u_tab.shape[0]
    n_keys = subk.shape[1]
    assert n_exp == n_keys * n_keys and n_keys % 16 == 0
    assert min(PEER_EB, n_exp) % (8 * n_keys) == 0
    assert PEER_SB % n_keys == 0 and min(PEER_EB, n_exp) % PEER_SB == 0
    assert n_tok % PEER_TT == 0 and n_lat % PEER_TT == 0 and l_len % PEER_TT == 0
    TT, EB = PEER_TT, min(PEER_EB, n_exp)
    lat_tiles = n_lat // TT
    per_b = l_len // TT

    def mod_map(i, e):
        return (jnp.where(i < lat_tiles, i // per_b, 0), (i < lat_tiles).astype(jnp.int32), 0, 0)

    n_blk = n_exp // EB
    vt_tab = jnp.transpose(v_tab.reshape(n_blk, EB, D_MODEL), (0, 2, 1))
    kern = functools.partial(_peer_kernel, n_keys=n_keys, final=final)
    full = lambda shape: pl.BlockSpec(shape, lambda i, e: (0,) * len(shape))
    return pl.pallas_call(
        kern,
        grid=(n_tok // TT, n_blk),
        in_specs=[pl.BlockSpec((TT, D_MODEL), lambda i, e: (i, 0)),
                  pl.BlockSpec((TT, D_MODEL), lambda i, e: (i, 0)),
                  pl.BlockSpec((1, 1, 6, D_MODEL), mod_map),
                  full(wq.shape), full(subk.shape),
                  pl.BlockSpec((EB, D_MODEL), lambda i, e: (e, 0)),
                  pl.BlockSpec((1, D_MODEL, EB), lambda i, e: (e, 0, 0)),
                  full((1, D_MODEL))],
        out_specs=pl.BlockSpec((TT, D_MODEL), lambda i, e: (i, 0)),
        out_shape=jax.ShapeDtypeStruct((n_tok, D_MODEL), F32),
        scratch_shapes=[pltpu.VMEM((D_MODEL, TT), BF16),
                        pltpu.VMEM((TT, PEER_HEADS * 256), BF16),
                        pltpu.VMEM((TT // LANES, PEER_HEADS, 2 * (n_keys + PEER_PAD), LANES), F32),
                        pltpu.VMEM((PEER_HEADS, n_keys, TT), F32),
                        pltpu.VMEM((PEER_HEADS, n_keys, TT), F32),
                        pltpu.VMEM((PEER_TOPK, TT), F32),
                        pltpu.VMEM((PEER_TOPK, TT), F32),
                        pltpu.VMEM((TT // LANES, EB, LANES), F32),
                        pltpu.VMEM((TT // LANES, EB, LANES), BF16),
                        pltpu.VMEM((D_MODEL, TT), F32)],
        compiler_params=_cparams(("arbitrary", "arbitrary")),
        name="peer_ffn",
    )(h2_all, x1_all, modsel, wq, subk, u_tab, vt_tab, fg)


def _head_slots(w, n_heads, width=HEAD_DIM):
    lead = w.shape[:-1]
    w = w.reshape(lead + (n_heads, width))
    w = jnp.pad(w, [(0, 0)] * len(lead) + [(0, 0), (0, HEAD_SLOT - width)])
    return w.reshape(lead + (n_heads * HEAD_SLOT,))


def _prep_w_in(w_in):
    sizes = (256, 128, 128, B_Q_LORA, B_KV_LORA, B_ROPE, 256, 256, 256, 256)
    offs = np.cumsum((0,) + sizes)
    aq, ak, av, bq, bkv, bkr, cu, dq, dk, dv = [w_in[:, offs[i]:offs[i + 1]] for i in range(10)]
    d = w_in.shape[0]
    bkr_p = jnp.concatenate([jnp.zeros((d, B_NOPE), F32), bkr, jnp.zeros((d, HEAD_SLOT - B_NOPE - B_ROPE), F32)], 1)
    cols = [_head_slots(aq, 4), _head_slots(ak, 2), _head_slots(av, 2),
            jnp.pad(bq, ((0, 0), (0, 256 - B_Q_LORA))), bkv, bkr_p, cu,
            _head_slots(dq, 4), _head_slots(dk, 4), _head_slots(dv, 4)]
    w = jnp.concatenate(cols, axis=1)
    assert w.shape[1] == _C_END
    return w.astype(BF16)


def _prep_mla(w_uq, w_ukv, qg, kvg):
    wq = _head_slots(w_uq, N_HEADS, B_NOPE + B_ROPE)
    wq = jnp.pad(wq, ((0, 256 - B_Q_LORA), (0, 0))).astype(BF16)
    kv = w_ukv.reshape(B_KV_LORA, N_HEADS, B_NOPE + HEAD_DIM)
    wk = _head_slots(kv[:, :, :B_NOPE].reshape(B_KV_LORA, -1), N_HEADS, B_NOPE)
    wv = _head_slots(kv[:, :, B_NOPE:].reshape(B_KV_LORA, -1), N_HEADS, HEAD_DIM)
    wkv = jnp.concatenate([wk, wv], axis=1).astype(BF16)
    qg_p = jnp.pad(qg, (0, 256 - B_Q_LORA)).reshape(1, 256)
    return wq, wkv, qg_p, kvg.reshape(1, B_KV_LORA)


def _rope_tables(ctx_len, l_len):
    t = np.arange(l_len)
    pos = np.stack([t // GRID_W, t % GRID_W], 0).astype(np.float64)

    def build(width, lane0):
        a = width // 2
        half = a // 2
        inv = ROPE_BASE ** (-np.arange(half, dtype=np.float64) / half)
        cos = np.ones((l_len, LANES)); sp = np.zeros((l_len, LANES)); sm = np.zeros((l_len, LANES))
        for j in range(width):
            axis, i = j // a, j % a
            f, second = i % half, i >= half
            ang = pos[axis] * inv[f]
            cos[:, lane0 + j] = np.cos(ang)
            if second:
                sp[:, lane0 + j] = np.sin(ang)
            else:
                sm[:, lane0 + j] = -np.sin(ang)
        tab = np.stack([cos, sp, sm], 0)
        ctx = np.stack([np.ones((ctx_len, LANES)), np.zeros((ctx_len, LANES)), np.zeros((ctx_len, LANES))], 0)
        return jnp.asarray(np.concatenate([ctx, tab], axis=1), F32)

    return build(HEAD_DIM, 0), build(B_ROPE, B_NOPE)


def _s5_matrices(lam_re, lam_im, log_step, b_re, b_im, c_re, c_im):
    lr, li = lam_re.astype(F32), lam_im.astype(F32)
    step = jnp.exp(log_step.astype(F32))[..., None]
    mag = jnp.exp(lr * step)
    abr, abi = mag * jnp.cos(li * step), mag * jnp.sin(li * step)
    nr, ni = abr - 1.0, abi
    den = lr * lr + li * li
    fr = (nr * lr + ni * li) / den
    fi = (ni * lr - nr * li) / den
    bbr = fr[..., None] * b_re - fi[..., None] * b_im
    bbi = fr[..., None] * b_im + fi[..., None] * b_re
    eye = jnp.eye(C_NGROUPS, dtype=F32)
    bm_r = jnp.einsum('dgpc,gh->dgchp', bbr, eye).reshape(2, GROUP_WIDTH, N_STATE)
    bm_i = jnp.einsum('dgpc,gh->dgchp', bbi, eye).reshape(2, GROUP_WIDTH, N_STATE)
    bmat = jnp.concatenate([bm_r, bm_i], axis=2).astype(BF16)
    cm_r = jnp.einsum('dgcp,gh->dgphc', c_re.astype(F32), eye).reshape(2, N_STATE, GROUP_WIDTH)
    cm_i = jnp.einsum('dgcp,gh->dgphc', c_im.astype(F32), eye).reshape(2, N_STATE, GROUP_WIDTH)
    cmat = jnp.concatenate([cm_r, -cm_i], axis=1).astype(BF16)
    return bmat, cmat, abr.reshape(2, 1, N_STATE), abi.reshape(2, 1, N_STATE)


def _na_bias(rpb):
    w = np.arange(GRID_W)
    cs = np.clip(w - NA_KW // 2, 0, GRID_W - NA_KW)
    c = np.arange(GRID_W)
    inwin = (c[None, :] >= cs[:, None]) & (c[None, :] < cs[:, None] + NA_KW)
    colidx = np.clip(c[None, :] - w[:, None] + NA_KW - 1, 0, 2 * NA_KW - 2)
    col_sel = (colidx[:, :, None] == np.arange(2 * NA_KW - 1)).astype(np.float32)
    d, y = np.arange(NA_KH)[:, None], np.arange(NA_KH)[None, :]
    row_sel = ((y - d + NA_KH - 1)[:, :, None] == np.arange(2 * NA_KH - 1)).astype(np.float32)
    hi = lax.Precision.HIGHEST
    t = jnp.einsum('hrq,wcq->hrwc', rpb.astype(F32), jnp.asarray(col_sel), precision=hi)
    b = jnp.einsum('dyr,hrwc->dhwyc', jnp.asarray(row_sel), t, precision=hi)
    b = jnp.where(jnp.asarray(inwin)[None, None, :, None, :], b, NEG_INF)
    b = b.reshape(NA_KH, N_HEADS, GRID_W, NA_KEYS)
    return jnp.concatenate([b, jnp.full((1, N_HEADS, GRID_W, NA_KEYS), NEG_INF, F32)], axis=0)


def _mix_layout(mix_norm_g, w_out):
    g = mix_norm_g.reshape(4, GROUP_WIDTH)
    w = w_out.reshape(4, GROUP_WIDTH, D_MODEL)
    gs, ws = [], []
    for k in range(4):
        if k == 2:
            gs.append(g[k]); ws.append(w[k])
        else:
            gs.append(_head_slots(g[k], N_HEADS))
            wk = w[k].reshape(N_HEADS, HEAD_DIM, D_MODEL)
            wk = jnp.pad(wk, ((0, 0), (0, HEAD_SLOT - HEAD_DIM), (0, 0))).reshape(N_HEADS * HEAD_SLOT, D_MODEL)
            ws.append(wk)
    return jnp.concatenate(gs).reshape(1, -1), jnp.concatenate(ws, axis=0).astype(BF16)


def kernel(x, c, ctx, c_ctx, norm1_g, norm2_g, w_ada, b_ada, w_in, swa_sink, mla_q_norm_g, mla_w_uq,
           mla_kv_norm_g, mla_w_ukv, s5_lambda_re, s5_lambda_im, s5_log_step, s5_b_re, s5_b_im, s5_c_re,
           s5_c_im, s5_d, s5_w_glu, na_rpb, mix_norm_g, w_out, peer_w_q, peer_sub_keys, peer_u, peer_v,
           final_norm_g):
    n_b, l_len, d = x.shape
    ctx_len = ctx.shape[1]
    depth = w_in.shape[0]
    s_len = ctx_len + l_len
    assert d == D_MODEL and ctx_len == TOKEN_TILE and n_b % 8 == 0 and n_b < 16
    assert l_len // GRID_W >= NA_KH and l_len % PEER_TT == 0
    n_lat = n_b * l_len

    cc = jnp.concatenate([c, c_ctx[None], jnp.zeros((16 - n_b - 1, d), F32)], axis=0)
    mod = _ada_call(cc, w_ada, b_ada).reshape(depth, 16, 6, d)
    lat_mod = mod[:, :n_b]
    ctx_mod = jnp.broadcast_to(mod[:, n_b:n_b + 1], lat_mod.shape)
    modsel = jnp.stack([ctx_mod, lat_mod], axis=2)

    tab_a, tab_b = _rope_tables(ctx_len, l_len)
    x_all = jnp.concatenate([x.reshape(n_lat, d), ctx.reshape(n_b * ctx_len, d)], axis=0)
    fg = final_norm_g.reshape(1, d)

    for l in range(depth):
        last = l == depth - 1
        w_in_p = _prep_w_in(w_in[l])
        wuq, wukv, qg, kvg = _prep_mla(mla_w_uq[l], mla_w_ukv[l], mla_q_norm_g[l], mla_kv_norm_g[l])
        qa, ka, va, qb, kb, vb, cu, qd, kd, vd = _proj_call(
            x_all, modsel[l], norm1_g[l].reshape(1, d), w_in_p, tab_a, tab_b, qg, wuq, kvg, wukv, n_b, s_len)

        ya = _swa_call(swa_sink[l], qa, ka, va, ctx_len)
        yb = _mla_call(qb, kb, vb, ctx_len)

        bmat, cmat, a_r, a_i = _s5_matrices(s5_lambda_re[l], s5_lambda_im[l], s5_log_step[l],
                                            s5_b_re[l], s5_b_im[l], s5_c_re[l], s5_c_im[l])
        cu_t = cu.reshape(s_len, n_b, GROUP_WIDTH)
        y_dirs = _s5_call(cu_t, bmat, cmat, a_r, a_i, ctx_len)
        y_dirs = y_dirs.reshape(2, s_len, n_b * GROUP_WIDTH)

        yd = _na_call(qd, kd, vd, _na_bias(na_rpb[l]), ctx_len)

        mixg_p, w_out_p = _mix_layout(mix_norm_g[l], w_out[l])
        x1_all, h2_all = _out_call(ya, yb, cu, y_dirs, s5_d[l].reshape(1, GROUP_WIDTH), s5_w_glu[l].astype(BF16),
                                   yd, x_all, modsel[l], mixg_p, w_out_p, norm2_g[l].reshape(1, d))

        n_tok = n_lat if last else x_all.shape[0]
        x_all = _peer_call(h2_all, x1_all, modsel[l], peer_w_q[l].astype(BF16), peer_sub_keys[l].astype(BF16),
                           peer_u[l].astype(BF16), peer_v[l].astype(BF16), fg, n_tok, n_lat, l_len, last)

    return x_all[:n_lat].reshape(n_b, l_len, d)
```
